```python
import jax
import jax.numpy as jnp
from jax import lax
import numpy as np

D_MODEL = 4096
BATCH = 2
SEQ = 4096
DEPTH = 2

GRID_W = 64
CTX_LEN = 256
HEAD_DIM = 128
N_Q_HEADS = D_MODEL // 256
N_KV_HEADS = N_Q_HEADS // 4
Q_GROUP = N_Q_HEADS // N_KV_HEADS
ATTN_WIDTH = N_Q_HEADS * HEAD_DIM
KV_WIDTH = N_KV_HEADS * HEAD_DIM
LRU_WIDTH = D_MODEL // 2
LRU_BLOCKS = 16
LRU_BLOCK = LRU_WIDTH // LRU_BLOCKS
CONV_W = 4
CONV_LEFT = 2
LRU_C = 8.0
FNET_GROUPS = 4
FNET_WIDTH = D_MODEL // 2
FNET_GROUP = FNET_WIDTH // FNET_GROUPS
N_BRANCH = 3
BRANCH_WIDTH = D_MODEL // 2
D_FF = 3 * D_MODEL // 2
N_MOD = 9
Q_BLOCK = 128
ROPE_THETA = 10000.0
EPS = 1e-6

Q0 = 0
K0 = Q0 + ATTN_WIDTH
V0 = K0 + KV_WIDTH
X0 = V0 + KV_WIDTH
Y0 = X0 + LRU_WIDTH
F0 = Y0 + LRU_WIDTH
G0 = F0 + FNET_WIDTH
IN_COLS = G0 + N_BRANCH * D_MODEL

kernel_name = 'hybrid_gqa_rglru_fnet_prefix_dit'


def rms_norm(x, g):
    xf = x.astype(jnp.float32)
    y = xf * lax.rsqrt(jnp.mean(xf * xf, axis=-1, keepdims=True) + EPS)
    return (y * g.astype(jnp.float32)).astype(x.dtype)


def modulate(h, shift, scale):
    return h * (1.0 + scale) + shift


def swiglu(h, w_in, w_out):
    a, b = jnp.split(h @ w_in, 2, axis=-1)
    return (jax.nn.silu(a) * b) @ w_out


def axial_angles(n_tokens):
    rows = n_tokens // GRID_W
    row = jnp.repeat(jnp.arange(rows, dtype=jnp.float32), GRID_W)
    col = jnp.tile(jnp.arange(GRID_W, dtype=jnp.float32), rows)
    axis_dim = HEAD_DIM // 2
    inv_freq = ROPE_THETA ** (-jnp.arange(0, axis_dim, 2, dtype=jnp.float32) / axis_dim)
    return row[:, None] * inv_freq, col[:, None] * inv_freq


def rope_half(x, ang):
    m = ang.shape[-1]
    cos = jnp.cos(ang)[:, None, :]
    sin = jnp.sin(ang)[:, None, :]
    x1, x2 = x[..., :m], x[..., m:]
    return jnp.concatenate([x1 * cos - x2 * sin, x2 * cos + x1 * sin], axis=-1)


def axial_rope(x, ang_row, ang_col):
    xf = x.astype(jnp.float32)
    half = HEAD_DIM // 2
    out = jnp.concatenate([rope_half(xf[..., :half], ang_row),
                           rope_half(xf[..., half:], ang_col)], axis=-1)
    return out.astype(x.dtype)


def attend(q, k, v):
    s = jnp.einsum('bqkgd,bskd->bkgqs', q, k).astype(jnp.float32) * (HEAD_DIM ** -0.5)
    p = jax.nn.softmax(s, axis=-1).astype(v.dtype)
    return jnp.einsum('bkgqs,bskd->bqkgd', p, v)


def blocked_attention(q, k, v):
    b, s = q.shape[0], q.shape[1]
    n_blocks = s // Q_BLOCK
    qb = jnp.swapaxes(q.reshape(b, n_blocks, Q_BLOCK, N_KV_HEADS, Q_GROUP, HEAD_DIM), 0, 1)
    out = lax.map(lambda blk: attend(blk, k, v), qb)
    return jnp.swapaxes(out, 0, 1).reshape(b, s, ATTN_WIDTH)


def short_conv(x, w, bias):
    t = x.shape[1]
    xp = jnp.pad(x, ((0, 0), (CONV_LEFT, CONV_W - 1 - CONV_LEFT), (0, 0)))
    out = bias
    for j in range(CONV_W):
        out = out + xp[:, j:j + t] * w[j]
    return out


def block_diag(x, w, bias):
    xb = x.reshape(x.shape[:-1] + (LRU_BLOCKS, LRU_BLOCK))
    return jnp.einsum('btnc,ncd->btnd', xb, w).reshape(x.shape) + bias


def rglru_coeffs(x, wa, ba, wx, bx, lam):
    r = jax.nn.sigmoid(block_diag(x, wa, ba)).astype(jnp.float32)
    i = jax.nn.sigmoid(block_diag(x, wx, bx)).astype(jnp.float32)
    log_a = -LRU_C * r * jax.nn.softplus(-lam.astype(jnp.float32))
    a = jnp.exp(log_a)
    b = jnp.sqrt(-jnp.expm1(2.0 * log_a)) * (i * x.astype(jnp.float32))
    return a, b


def linear_scan(a, b, h0, reverse):
    def combine(left, right):
        return left[0] * right[0], right[0] * left[1] + right[1]
    a_cum, b_cum = lax.associative_scan(combine, (a, b), reverse=reverse, axis=1)
    h = b_cum + a_cum * h0[:, None, :]
    final = h[:, 0] if reverse else h[:, -1]
    return h, final


def fourier_mix(f):
    b, t = f.shape[0], f.shape[1]
    fg = f.astype(jnp.float32).reshape(b, t, FNET_GROUPS, FNET_GROUP)
    mixed = jnp.fft.fft2(fg, axes=(1, 3), norm='ortho').real
    return mixed.reshape(b, t, FNET_WIDTH).astype(f.dtype)


def gated_merge(branches, gate_logits, b_gate, w_branch, w_out):
    b, t = gate_logits.shape[0], gate_logits.shape[1]
    g = jax.nn.sigmoid(gate_logits.reshape(b, t, N_BRANCH, D_MODEL) + b_gate)
    merged = g[:, :, 0] * (branches[0] @ w_branch[0])
    for j in range(1, N_BRANCH):
        merged = merged + g[:, :, j] * (branches[j] @ w_branch[j])
    return merged @ w_out


def mixer(h, hc, ang_row, ang_col, w_in, b_gate, q_gain, k_gain, conv_w, conv_b,
          lru_wa, lru_ba, lru_wx, lru_bx, lru_lam, w_branch, w_out, with_ctx_out):
    b, s = h.shape[0], h.shape[1]
    tc = hc.shape[1]
    p = h @ w_in
    if with_ctx_out:
        pc = hc @ w_in
        pc_kvx = pc[..., K0:Y0]
    else:
        pc_kvx = hc @ w_in[:, K0:Y0]

    kc = rms_norm(pc_kvx[..., :KV_WIDTH].reshape(b, tc, N_KV_HEADS, HEAD_DIM), k_gain)
    vc = pc_kvx[..., KV_WIDTH:2 * KV_WIDTH].reshape(b, tc, N_KV_HEADS, HEAD_DIM)
    xc_in = short_conv(pc_kvx[..., 2 * KV_WIDTH:], conv_w, conv_b)

    q = rms_norm(p[..., Q0:K0].reshape(b, s, N_Q_HEADS, HEAD_DIM), q_gain)
    k = rms_norm(p[..., K0:V0].reshape(b, s, N_KV_HEADS, HEAD_DIM), k_gain)
    v = p[..., V0:X0].reshape(b, s, N_KV_HEADS, HEAD_DIM)
    q = axial_rope(q, ang_row, ang_col).reshape(b, s, N_KV_HEADS, Q_GROUP, HEAD_DIM)
    k = axial_rope(k, ang_row, ang_col)
    attn = blocked_attention(q, jnp.concatenate([k, kc], axis=1), jnp.concatenate([v, vc], axis=1))

    x_in = short_conv(p[..., X0:Y0], conv_w, conv_b)
    h_lat = []
    h_ctx = []
    for d, rev in ((0, False), (1, True)):
        a_c, b_c = rglru_coeffs(xc_in, lru_wa[d], lru_ba[d], lru_wx[d], lru_bx[d], lru_lam[d])
        hc_seq, hc_final = linear_scan(a_c, b_c, jnp.zeros((b, LRU_WIDTH), jnp.float32), rev)
        a_l, b_l = rglru_coeffs(x_in, lru_wa[d], lru_ba[d], lru_wx[d], lru_bx[d], lru_lam[d])
        hl_seq, _ = linear_scan(a_l, b_l, hc_final, rev)
        h_lat.append(hl_seq)
        h_ctx.append(hc_seq)
    lru = (h_lat[0] + h_lat[1]).astype(h.dtype) * jax.nn.gelu(p[..., Y0:F0])

    four = fourier_mix(p[..., F0:G0])

    y = gated_merge((attn, lru, four), p[..., G0:], b_gate, w_branch, w_out)
    if not with_ctx_out:
        return y, None

    qc = rms_norm(pc[..., Q0:K0].reshape(b, tc, N_Q_HEADS, HEAD_DIM), q_gain)
    qc = qc.reshape(b, tc, N_KV_HEADS, Q_GROUP, HEAD_DIM)
    attn_c = attend(qc, kc, vc).reshape(b, tc, ATTN_WIDTH)
    lru_c = (h_ctx[0] + h_ctx[1]).astype(hc.dtype) * jax.nn.gelu(pc[..., Y0:F0])
    four_c = fourier_mix(pc[..., F0:G0])
    yc = gated_merge((attn_c, lru_c, four_c), pc[..., G0:], b_gate, w_branch, w_out)
    return y, yc


def setup_inputs(seed: int = 0) -> dict:
    key = jax.random.key(seed)
    ks = jax.random.split(key, 24)
    f32 = jnp.float32

    def nrm(k, shape, scale):
        return jax.random.normal(k, shape, f32) * scale

    u = jax.random.uniform(ks[19], (DEPTH, 2, LRU_WIDTH), f32, 0.9, 0.999)
    a0 = u ** (1.0 / LRU_C)
    lru_lam = jnp.log(a0) - jnp.log1p(-a0)
    return {
        'x': nrm(ks[0], (BATCH, SEQ, D_MODEL), 1.0),
        'c': nrm(ks[1], (BATCH, D_MODEL), 1.0),
        'ctx': nrm(ks[2], (BATCH, CTX_LEN, D_MODEL), 1.0),
        'c_ctx': nrm(ks[3], (D_MODEL,), 1.0),
        'w_ada': nrm(ks[4], (DEPTH, D_MODEL, N_MOD * D_MODEL), 0.5 * D_MODEL ** -0.5),
        'b_ada': nrm(ks[5], (DEPTH, N_MOD * D_MODEL), 0.02),
        'g_norm': 1.0 + nrm(ks[6], (DEPTH, 3, D_MODEL), 0.02),
        'w_ff_in': nrm(ks[7], (DEPTH, 2, D_MODEL, 2 * D_FF), D_MODEL ** -0.5),
        'w_ff_out': nrm(ks[8], (DEPTH, 2, D_FF, D_MODEL), D_FF ** -0.5),
        'w_in': nrm(ks[9], (DEPTH, D_MODEL, IN_COLS), D_MODEL ** -0.5),
        'b_gate': nrm(ks[10], (DEPTH, N_BRANCH, D_MODEL), 0.02),
        'q_gain': 1.0 + nrm(ks[11], (DEPTH, HEAD_DIM), 0.02),
        'k_gain': 1.0 + nrm(ks[12], (DEPTH, HEAD_DIM), 0.02),
        'conv_w': nrm(ks[13], (DEPTH, CONV_W, LRU_WIDTH), CONV_W ** -0.5),
        'conv_b': nrm(ks[14], (DEPTH, LRU_WIDTH), 0.02),
        'lru_wa': nrm(ks[15], (DEPTH, 2, LRU_BLOCKS, LRU_BLOCK, LRU_BLOCK), LRU_BLOCK ** -0.5),
        'lru_ba': nrm(ks[16], (DEPTH, 2, LRU_WIDTH), 0.02),
        'lru_wx': nrm(ks[17], (DEPTH, 2, LRU_BLOCKS, LRU_BLOCK, LRU_BLOCK), LRU_BLOCK ** -0.5),
        'lru_bx': nrm(ks[18], (DEPTH, 2, LRU_WIDTH), 0.02),
        'lru_lam': lru_lam,
        'w_branch': nrm(ks[20], (DEPTH, N_BRANCH, BRANCH_WIDTH, D_MODEL), BRANCH_WIDTH ** -0.5),
        'w_out': nrm(ks[21], (DEPTH, D_MODEL, D_MODEL), D_MODEL ** -0.5),
        'g_final': 1.0 + nrm(ks[22], (D_MODEL,), 0.02),
    }


def reference(x, c, ctx, c_ctx, w_ada, b_ada, g_norm, w_ff_in, w_ff_out, w_in, b_gate,
              q_gain, k_gain, conv_w, conv_b, lru_wa, lru_ba, lru_wx, lru_bx, lru_lam,
              w_branch, w_out, g_final):
    b = x.shape[0]
    ang_row, ang_col = axial_angles(x.shape[1])
    xc = ctx
    for l in range(DEPTH):
        last = l == DEPTH - 1
        mod = (jax.nn.silu(c) @ w_ada[l] + b_ada[l]).reshape(b, 1, N_MOD, D_MODEL)
        modc = (jax.nn.silu(c_ctx) @ w_ada[l] + b_ada[l]).reshape(1, 1, N_MOD, D_MODEL)

        x = x + 0.5 * mod[:, :, 2] * swiglu(
            modulate(rms_norm(x, g_norm[l, 0]), mod[:, :, 0], mod[:, :, 1]), w_ff_in[l, 0], w_ff_out[l, 0])
        xc = xc + 0.5 * modc[:, :, 2] * swiglu(
            modulate(rms_norm(xc, g_norm[l, 0]), modc[:, :, 0], modc[:, :, 1]), w_ff_in[l, 0], w_ff_out[l, 0])

        h = modulate(rms_norm(x, g_norm[l, 1]), mod[:, :, 3], mod[:, :, 4])
        hc = modulate(rms_norm(xc, g_norm[l, 1]), modc[:, :, 3], modc[:, :, 4])
        y, yc = mixer(h, hc, ang_row, ang_col, w_in[l], b_gate[l], q_gain[l], k_gain[l],
                      conv_w[l], conv_b[l], lru_wa[l], lru_ba[l], lru_wx[l], lru_bx[l], lru_lam[l],
                      w_branch[l], w_out[l], not last)
        x = x + mod[:, :, 5] * y

        x = x + 0.5 * mod[:, :, 8] * swiglu(
            modulate(rms_norm(x, g_norm[l, 2]), mod[:, :, 6], mod[:, :, 7]), w_ff_in[l, 1], w_ff_out[l, 1])
        if not last:
            xc = xc + modc[:, :, 5] * yc
            xc = xc + 0.5 * modc[:, :, 8] * swiglu(
                modulate(rms_norm(xc, g_norm[l, 2]), modc[:, :, 6], modc[:, :, 7]), w_ff_in[l, 1], w_ff_out[l, 1])
    return rms_norm(x, g_final)
```

```python
import functools
import math

import jax
import jax.numpy as jnp
from jax import lax
from jax.experimental import pallas as pl
from jax.experimental.pallas import tpu as pltpu

F32 = jnp.float32
BF = jnp.bfloat16

VMEM_LIMIT_BYTES = 56 * 1024 * 1024
LANES = 128

HEAD_DIM = 128
Q_GROUP = 4
GRID_W = 64
CONV_W = 4
CONV_LEFT = 2
LRU_C = 8.0
LRU_BLOCK = 128
FNET_GROUPS = 4
N_BRANCH = 3
N_MOD = 9
ROPE_THETA = 10000.0
EPS = 1e-6
CONV_PAD = 8


def _cparams(*sem):
    return pltpu.CompilerParams(dimension_semantics=sem, vmem_limit_bytes=VMEM_LIMIT_BYTES)


def _silu(a):
    return a * jax.nn.sigmoid(a)


def _ada_kernel(c_ref, w_ref, b_ref, o_ref):
    s = _silu(c_ref[...]).astype(BF)
    o_ref[...] = jnp.dot(s, w_ref[...].astype(BF), preferred_element_type=F32) + b_ref[...]


def ada_mod(c_rows, w_ada, b_ada, tn=1024):
    depth, d, n = w_ada.shape
    rows = c_rows.shape[0]
    return pl.pallas_call(
        _ada_kernel,
        out_shape=jax.ShapeDtypeStruct((depth, rows, n), F32),
        grid=(depth, n // tn),
        in_specs=[pl.BlockSpec((rows, d), lambda l, j: (0, 0)),
                  pl.BlockSpec((None, d, tn), lambda l, j: (l, 0, j)),
                  pl.BlockSpec((None, 1, tn), lambda l, j: (l, 0, j))],
        out_specs=pl.BlockSpec((None, rows, tn), lambda l, j: (l, 0, j)),
        compiler_params=_cparams("arbitrary", "arbitrary"),
        name="ada_mod",
    )(c_rows, w_ada, b_ada.reshape(depth, 1, n))


def _norm_mod_kernel(x_ref, g_ref, sh_ref, sc_ref, o_ref):
    x = x_ref[...]
    ms = jnp.mean(x * x, axis=-1, keepdims=True)
    y = x * lax.rsqrt(ms + EPS) * g_ref[...]
    o_ref[...] = (y * (1.0 + sc_ref[...]) + sh_ref[...]).astype(o_ref.dtype)


def norm_mod(x, g, mod, k_shift, k_scale, ts=256):
    G, S, D = x.shape
    ts = min(ts, S)
    return pl.pallas_call(
        _norm_mod_kernel,
        out_shape=jax.ShapeDtypeStruct((G, S, D), BF),
        grid=(G, S // ts),
        in_specs=[pl.BlockSpec((None, ts, D), lambda g_, i: (g_, i, 0)),
                  pl.BlockSpec((1, D), lambda g_, i: (0, 0)),
                  pl.BlockSpec((None, None, 1, D), lambda g_, i: (g_, k_shift, 0, 0)),
                  pl.BlockSpec((None, None, 1, D), lambda g_, i: (g_, k_scale, 0, 0))],
        out_specs=pl.BlockSpec((None, ts, D), lambda g_, i: (g_, i, 0)),
        compiler_params=_cparams("arbitrary", "arbitrary"),
        name="norm_mod",
    )(x, g.reshape(1, D), mod, mod)


def _final_norm_kernel(x_ref, g_ref, o_ref):
    x = x_ref[...]
    ms = jnp.mean(x * x, axis=-1, keepdims=True)
    o_ref[...] = x * lax.rsqrt(ms + EPS) * g_ref[...]


def final_norm(x, g, ts=256):
    G, S, D = x.shape
    return pl.pallas_call(
        _final_norm_kernel,
        out_shape=jax.ShapeDtypeStruct((G, S, D), F32),
        grid=(G, S // ts),
        in_specs=[pl.BlockSpec((None, ts, D), lambda g_, i: (g_, i, 0)),
                  pl.BlockSpec((1, D), lambda g_, i: (0, 0))],
        out_specs=pl.BlockSpec((None, ts, D), lambda g_, i: (g_, i, 0)),
        compiler_params=_cparams("arbitrary", "arbitrary"),
        name="final_norm",
    )(x, g.reshape(1, D))


def _swiglu_kernel(h_ref, wa_ref, wb_ref, o_ref):
    h = h_ref[...]
    a = jnp.dot(h, wa_ref[...], preferred_element_type=F32)
    b = jnp.dot(h, wb_ref[...], preferred_element_type=F32)
    o_ref[...] = (_silu(a) * b).astype(o_ref.dtype)


def swiglu_in(h, w, l, i, tm, tn=512):
    G, S, K = h.shape
    F = w.shape[-1] // 2
    nb = F // tn
    return pl.pallas_call(
        _swiglu_kernel,
        out_shape=jax.ShapeDtypeStruct((G, S, F), BF),
        grid=(G, S // tm, nb),
        in_specs=[pl.BlockSpec((None, tm, K), lambda g, m, n: (g, m, 0)),
                  pl.BlockSpec((None, None, K, tn), lambda g, m, n: (l, i, 0, n)),
                  pl.BlockSpec((None, None, K, tn), lambda g, m, n: (l, i, 0, n + nb))],
        out_specs=pl.BlockSpec((None, tm, tn), lambda g, m, n: (g, m, n)),
        compiler_params=_cparams("arbitrary", "arbitrary", "arbitrary"),
        name="swiglu_in",
    )(h, w, w)


def _resid_kernel(coef, a_ref, w_ref, x_ref, g_ref, o_ref):
    y = jnp.dot(a_ref[...], w_ref[...], preferred_element_type=F32)
    o_ref[...] = x_ref[...] + (coef * g_ref[...]) * y


def resid_out(a, w, w_idx, x, mod, k_gate, coef, tm, tn=512):
    G, S, K = a.shape
    D = x.shape[-1]
    nlead = len(w_idx)
    w_block = (None,) * nlead + (K, tn)
    return pl.pallas_call(
        functools.partial(_resid_kernel, coef),
        out_shape=jax.ShapeDtypeStruct((G, S, D), F32),
        grid=(G, S // tm, D // tn),
        in_specs=[pl.BlockSpec((None, tm, K), lambda g, m, n: (g, m, 0)),
                  pl.BlockSpec(w_block, lambda g, m, n: tuple(w_idx) + (0, n)),
                  pl.BlockSpec((None, tm, tn), lambda g, m, n: (g, m, n)),
                  pl.BlockSpec((None, None, 1, tn), lambda g, m, n: (g, k_gate, 0, n))],
        out_specs=pl.BlockSpec((None, tm, tn), lambda g, m, n: (g, m, n)),
        compiler_params=_cparams("arbitrary", "arbitrary", "arbitrary"),
        name="resid_out",
    )(a, w, x, mod)


def _proj_kernel(act, h_ref, w_ref, *rest):
    o_ref = rest[-1]
    acc = jnp.dot(h_ref[...], w_ref[...], preferred_element_type=F32)
    if act == "gelu":
        acc = jax.nn.gelu(acc)
    elif act == "sigmoid_bias":
        acc = jax.nn.sigmoid(acc + rest[0][...])
    o_ref[...] = acc.astype(o_ref.dtype)


def proj(h, w_in, l, col0, ncols, out_dtype, tm, act="none", bias=None, tn=512):
    G, S, K = h.shape
    cb = col0 // tn
    in_specs = [pl.BlockSpec((None, tm, K), lambda g, m, n: (g, m, 0)),
                pl.BlockSpec((None, K, tn), lambda g, m, n: (l, 0, cb + n))]
    args = [h, w_in]
    if bias is not None:
        in_specs.append(pl.BlockSpec((1, tn), lambda g, m, n: (0, n)))
        args.append(bias)
    return pl.pallas_call(
        functools.partial(_proj_kernel, act),
        out_shape=jax.ShapeDtypeStruct((G, S, ncols), out_dtype),
        grid=(G, S // tm, ncols // tn),
        in_specs=in_specs,
        out_specs=pl.BlockSpec((None, tm, tn), lambda g, m, n: (g, m, n)),
        compiler_params=_cparams("arbitrary", "arbitrary", "arbitrary"),
        name="proj_" + act,
    )(*args)


def _qk_kernel(h_ref, w_ref, gain_ref, cos_ref, sin_ref, o_ref):
    acc = jnp.dot(h_ref[...], w_ref[...], preferred_element_type=F32)
    cos = cos_ref[...]
    sin = sin_ref[...]
    lane = lax.broadcasted_iota(jnp.int32, (1, HEAD_DIM), 1)
    low_half = (lane % (HEAD_DIM // 2)) < (HEAD_DIM // 4)
    for hd in range(acc.shape[1] // HEAD_DIM):
        cols = slice(hd * HEAD_DIM, (hd + 1) * HEAD_DIM)
        v = acc[:, cols]
        ms = jnp.mean(v * v, axis=-1, keepdims=True)
        y = v * lax.rsqrt(ms + EPS) * gain_ref[:, cols]
        partner = jnp.where(low_half, pltpu.roll(y, HEAD_DIM - HEAD_DIM // 4, 1),
                            pltpu.roll(y, HEAD_DIM // 4, 1))
        o_ref[:, cols] = (y * cos + partner * sin).astype(o_ref.dtype)


def proj_qk(h, w_in, l, ncols, gain, cos, sin, rows_per_seq, tm, tn=512):
    G, S, K = h.shape
    seq_blocks = rows_per_seq // tm
    return pl.pallas_call(
        _qk_kernel,
        out_shape=jax.ShapeDtypeStruct((G, S, ncols), BF),
        grid=(G, S // tm, ncols // tn),
        in_specs=[pl.BlockSpec((None, tm, K), lambda g, m, n: (g, m, 0)),
                  pl.BlockSpec((None, K, tn), lambda g, m, n: (l, 0, n)),
                  pl.BlockSpec((1, tn), lambda g, m, n: (0, n)),
                  pl.BlockSpec((tm, HEAD_DIM), lambda g, m, n: (m % seq_blocks, 0)),
                  pl.BlockSpec((tm, HEAD_DIM), lambda g, m, n: (m % seq_blocks, 0))],
        out_specs=pl.BlockSpec((None, tm, tn), lambda g, m, n: (g, m, n)),
        compiler_params=_cparams("arbitrary", "arbitrary", "arbitrary"),
        name="proj_qk",
    )(h, w_in, gain, cos, sin)


def _attn_kernel(n_lat_chunks, tc, q_ref, kc_ref, vc_ref, *rest):
    o_ref = rest[-1]
    tq = q_ref.shape[0]
    scale = HEAD_DIM ** -0.5
    q = jnp.concatenate([q_ref[:, g * HEAD_DIM:(g + 1) * HEAD_DIM] for g in range(Q_GROUP)], axis=0)

    def step(k, v, carry):
        m, l, acc = carry
        s = lax.dot_general(q, k, (((1,), (1,)), ((), ())), preferred_element_type=F32) * scale
        m_new = jnp.maximum(m, jnp.max(s, axis=-1, keepdims=True))
        alpha = jnp.exp(m - m_new)
        p = jnp.exp(s - m_new)
        l = alpha * l + jnp.sum(p, axis=-1, keepdims=True)
        acc = alpha * acc + jnp.dot(p.astype(BF), v, preferred_element_type=F32)
        return m_new, l, acc

    rows = Q_GROUP * tq
    carry = (jnp.full((rows, 1), -1e30, F32), jnp.zeros((rows, 1), F32), jnp.zeros((rows, HEAD_DIM), F32))
    carry = step(kc_ref[...], vc_ref[...], carry)
    if n_lat_chunks:
        kl_ref, vl_ref = rest[0], rest[1]

        def body(c, cr):
            off = pl.multiple_of(c * tc, tc)
            return step(kl_ref[pl.ds(off, tc), :], vl_ref[pl.ds(off, tc), :], cr)

        carry = lax.fori_loop(0, n_lat_chunks, body, carry)
    _, l, acc = carry
    out = acc / l
    for g in range(Q_GROUP):
        o_ref[:, g * HEAD_DIM:(g + 1) * HEAD_DIM] = out[g * tq:(g + 1) * tq].astype(o_ref.dtype)


def attention(qk, v, qk_ctx, v_ctx, attn_width, with_latent_keys, tq=128, tc=512):
    B, S, _ = qk.shape
    Tc = qk_ctx.shape[1]
    n_kv = v.shape[-1] // HEAD_DIM
    gw = Q_GROUP * HEAD_DIM
    kcol = attn_width // HEAD_DIM
    tq = min(tq, S)
    tc = min(tc, S)
    in_specs = [pl.BlockSpec((None, tq, gw), lambda b, h, i: (b, i, h)),
                pl.BlockSpec((None, Tc, HEAD_DIM), lambda b, h, i: (b, 0, kcol + h)),
                pl.BlockSpec((None, Tc, HEAD_DIM), lambda b, h, i: (b, 0, h))]
    args = [qk, qk_ctx, v_ctx]
    n_chunks = 0
    if with_latent_keys:
        n_chunks = S // tc
        in_specs += [pl.BlockSpec((None, S, HEAD_DIM), lambda b, h, i: (b, 0, kcol + h)),
                     pl.BlockSpec((None, S, HEAD_DIM), lambda b, h, i: (b, 0, h))]
        args += [qk, v]
    return pl.pallas_call(
        functools.partial(_attn_kernel, n_chunks, tc),
        out_shape=jax.ShapeDtypeStruct((B, S, attn_width), BF),
        grid=(B, n_kv, S // tq),
        in_specs=in_specs,
        out_specs=pl.BlockSpec((None, tq, gw), lambda b, h, i: (b, i, h)),
        compiler_params=_cparams("arbitrary", "arbitrary", "arbitrary"),
        name="attention",
    )(*args)


def _neg_expm1(y):
    series = y * (1.0 / 9.0)
    for k in range(8, 0, -1):
        series = (y * (1.0 / k)) * (1.0 + series)
    return jnp.where(y > -0.25, -series, 1.0 - jnp.exp(y))


def _lru_kernel(S, tchunk, x_ref, gy_ref, h0_ref, cw_ref, cb_ref, wa_ref, ba_ref, wx_ref, bx_ref,
                lam_ref, o_ref, hfin_ref, xs_ref, a_ref, b_ref):
    zeros = jnp.zeros((CONV_PAD, LRU_BLOCK), F32)
    xs_ref[0:CONV_PAD, :] = zeros
    xs_ref[CONV_PAD + S:2 * CONV_PAD + S, :] = zeros
    xs_ref[CONV_PAD:CONV_PAD + S, :] = x_ref[...]

    neg_lam = -lam_ref[...]
    softplus = jnp.maximum(neg_lam, 0.0) + jnp.log1p(jnp.exp(-jnp.abs(neg_lam)))

    for c in range(S // tchunk):
        t0 = c * tchunk
        xc = cb_ref[...]
        for j in range(CONV_W):
            xc = xc + xs_ref[pl.ds(CONV_PAD + t0 + j - CONV_LEFT, tchunk), :] * cw_ref[j:j + 1, :]
        xb = xc.astype(BF)
        for d in range(2):
            r = jax.nn.sigmoid(jnp.dot(xb, wa_ref[d], preferred_element_type=F32) + ba_ref[d:d + 1, :])
            i = jax.nn.sigmoid(jnp.dot(xb, wx_ref[d], preferred_element_type=F32) + bx_ref[d:d + 1, :])
            log_a = -LRU_C * r * softplus[d:d + 1, :]
            a_ref[d, pl.ds(t0, tchunk), :] = jnp.exp(log_a)
            b_ref[d, pl.ds(t0, tchunk), :] = jnp.sqrt(_neg_expm1(2.0 * log_a)) * (i * xc)

    def body(t, carry):
        hf, hr = carry
        hf = a_ref[0, pl.ds(t, 1), :] * hf + b_ref[0, pl.ds(t, 1), :]
        b_ref[0, pl.ds(t, 1), :] = hf
        tr = S - 1 - t
        hr = a_ref[1, pl.ds(tr, 1), :] * hr + b_ref[1, pl.ds(tr, 1), :]
        b_ref[1, pl.ds(tr, 1), :] = hr
        return hf, hr

    hf, hr = lax.fori_loop(0, S, body, (h0_ref[0:1, :], h0_ref[1:2, :]), unroll=8)
    hfin_ref[0:1, :] = hf
    hfin_ref[1:2, :] = hr
    o_ref[...] = ((b_ref[0] + b_ref[1]) * gy_ref[...]).astype(o_ref.dtype)


def lru_mix(xseg, gy, h0, conv_w, conv_b, wa, ba, wx, bx, lam, l, tchunk=512):
    B, S, W = xseg.shape
    tchunk = min(tchunk, S)
    nblk = W // LRU_BLOCK
    seq = pl.BlockSpec((None, S, LRU_BLOCK), lambda b, n: (b, 0, n))
    st = pl.BlockSpec((None, 2, LRU_BLOCK), lambda b, n: (b, 0, n))
    vec2 = pl.BlockSpec((None, 2, LRU_BLOCK), lambda b, n: (l, 0, n))
    wblk = pl.BlockSpec((None, 2, None, LRU_BLOCK, LRU_BLOCK), lambda b, n: (l, 0, n, 0, 0))
    return pl.pallas_call(
        functools.partial(_lru_kernel, S, tchunk),
        out_shape=(jax.ShapeDtypeStruct((B, S, W), BF), jax.ShapeDtypeStruct((B, 2, W), F32)),
        grid=(B, nblk),
        in_specs=[seq, seq, st,
                  pl.BlockSpec((None, CONV_W, LRU_BLOCK), lambda b, n: (l, 0, n)),
                  pl.BlockSpec((None, 1, LRU_BLOCK), lambda b, n: (l, 0, n)),
                  wblk, vec2, wblk, vec2, vec2],
        out_specs=(seq, st),
        scratch_shapes=[pltpu.VMEM((S + 2 * CONV_PAD, LRU_BLOCK), F32),
                        pltpu.VMEM((2, S, LRU_BLOCK), F32),
                        pltpu.VMEM((2, S, LRU_BLOCK), F32)],
        compiler_params=_cparams("arbitrary", "arbitrary"),
        name="lru_mix",
    )(xseg, gy, h0, conv_w, conv_b.reshape(conv_b.shape[0], 1, W), wa, ba, wx, bx, lam)


def _dft_tables(n):
    k = jnp.arange(n, dtype=jnp.int32)
    ang = ((k[:, None] * k[None, :]) % n).astype(F32) * (2.0 * math.pi / n)
    return jnp.cos(ang), jnp.sin(ang)


def _chan_dft_kernel(f_ref, cs_ref, zc_ref, zs_ref):
    w = zc_ref.shape[-1]
    z = jnp.dot(f_ref[...], cs_ref[...], preferred_element_type=F32)
    zc_ref[...] = z[:, :w].astype(zc_ref.dtype)
    zs_ref[...] = z[:, w:].astype(zs_ref.dtype)


def chan_dft(f, cs, tm):
    G, S, W = f.shape
    gw = W // FNET_GROUPS
    blk = pl.BlockSpec((None, tm, gw), lambda g, m, n: (g, m, n))
    return pl.pallas_call(
        _chan_dft_kernel,
        out_shape=(jax.ShapeDtypeStruct((G, S, W), BF),) * 2,
        grid=(G, S // tm, FNET_GROUPS),
        in_specs=[blk, pl.BlockSpec((gw, 2 * gw), lambda g, m, n: (0, 0))],
        out_specs=(blk, blk),
        compiler_params=_cparams("arbitrary", "arbitrary", "arbitrary"),
        name="chan_dft",
    )(f, cs)


def _time_dft_kernel(scale, c_ref, s_ref, zc_ref, zs_ref, o_ref):
    re = jnp.dot(c_ref[...], zc_ref[...], preferred_element_type=F32)
    re = re - jnp.dot(s_ref[...], zs_ref[...], preferred_element_type=F32)
    o_ref[...] = (re * scale).astype(o_ref.dtype)


def time_dft(zc, zs, ct, st, scale, tm=512, tn=512):
    B, T, W = zc.shape
    tm = min(tm, T)
    a_spec = pl.BlockSpec((tm, T), lambda b, n, m: (m, 0))
    z_spec = pl.BlockSpec((None, T, tn), lambda b, n, m: (b, 0, n))
    return pl.pallas_call(
        functools.partial(_time_dft_kernel, scale),
        out_shape=jax.ShapeDtypeStruct((B, T, W), BF),
        grid=(B, W // tn, T // tm),
        in_specs=[a_spec, a_spec, z_spec, z_spec],
        out_specs=pl.BlockSpec((None, tm, tn), lambda b, n, m: (b, m, n)),
        compiler_params=_cparams("arbitrary", "arbitrary", "arbitrary"),
        name="time_dft",
    )(ct, st, zc, zs)


def _merge_kernel(b0_ref, b1_ref, b2_ref, w_ref, g0_ref, g1_ref, g2_ref, o_ref):
    acc = g0_ref[...] * jnp.dot(b0_ref[...], w_ref[0], preferred_element_type=F32)
    acc = acc + g1_ref[...] * jnp.dot(b1_ref[...], w_ref[1], preferred_element_type=F32)
    acc = acc + g2_ref[...] * jnp.dot(b2_ref[...], w_ref[2], preferred_element_type=F32)
    o_ref[...] = acc.astype(o_ref.dtype)


def gated_merge(branches, w_branch, l, gates, tm, tn=256):
    G, S, K = branches[0].shape
    D = w_branch.shape[-1]
    nb = D // tn
    b_spec = pl.BlockSpec((None, tm, K), lambda g, m, n: (g, m, 0))

    def g_spec(j):
        return pl.BlockSpec((None, tm, tn), lambda g, m, n: (g, m, j * nb + n))

    return pl.pallas_call(
        _merge_kernel,
        out_shape=jax.ShapeDtypeStruct((G, S, D), BF),
        grid=(G, S // tm, nb),
        in_specs=[b_spec, b_spec, b_spec,
                  pl.BlockSpec((None, N_BRANCH, K, tn), lambda g, m, n: (l, 0, 0, n)),
                  g_spec(0), g_spec(1), g_spec(2)],
        out_specs=pl.BlockSpec((None, tm, tn), lambda g, m, n: (g, m, n)),
        compiler_params=_cparams("arbitrary", "arbitrary", "arbitrary"),
        name="gated_merge",
    )(*branches, w_branch, gates, gates, gates)


def _rope_tables(n_tokens):
    rows = n_tokens // GRID_W
    row = jnp.repeat(jnp.arange(rows, dtype=F32), GRID_W)
    col = jnp.tile(jnp.arange(GRID_W, dtype=F32), rows)
    axis_dim = HEAD_DIM // 2
    inv_freq = ROPE_THETA ** (-jnp.arange(0, axis_dim, 2, dtype=F32) / axis_dim)
    ar, ac = row[:, None] * inv_freq, col[:, None] * inv_freq
    cos = jnp.concatenate([jnp.cos(ar), jnp.cos(ar), jnp.cos(ac), jnp.cos(ac)], axis=-1)
    sin = jnp.concatenate([-jnp.sin(ar), jnp.sin(ar), -jnp.sin(ac), jnp.sin(ac)], axis=-1)
    return cos, sin


def kernel(x, c, ctx, c_ctx, w_ada, b_ada, g_norm, w_ff_in, w_ff_out, w_in, b_gate, q_gain, k_gain,
           conv_w, conv_b, lru_wa, lru_ba, lru_wx, lru_bx, lru_lam, w_branch, w_out, g_final):
    B, S, D = x.shape
    Tc = ctx.shape[1]
    depth = w_ada.shape[0]
    lru_w = conv_w.shape[-1]
    fnet_w = D // 2
    attn_w = (D // 256) * HEAD_DIM
    kv_w = attn_w // Q_GROUP
    k0 = attn_w
    v0 = k0 + kv_w
    x0 = v0 + kv_w
    y0 = x0 + lru_w
    f0 = y0 + lru_w
    g0 = f0 + fnet_w

    w_ff_in_b = w_ff_in.astype(BF)
    w_ff_out_b = w_ff_out.astype(BF)
    w_in_b = w_in.astype(BF)
    w_branch_b = w_branch.astype(BF)
    w_out_b = w_out.astype(BF)
    lru_wa_b = lru_wa.astype(BF)
    lru_wx_b = lru_wx.astype(BF)

    n_rows = 8
    c_rows = jnp.concatenate([c, c_ctx[None, :], jnp.zeros((n_rows - B - 1, D), F32)], axis=0)
    mod = ada_mod(c_rows, w_ada, b_ada).reshape(depth, n_rows, N_MOD, 1, D)
    mod_lat = mod[:, :B]
    mod_ctx = mod[:, B:B + 1]

    cos_l, sin_l = _rope_tables(S)
    cos_c, sin_c = jnp.ones((B * Tc, HEAD_DIM), F32), jnp.zeros((B * Tc, HEAD_DIM), F32)
    gain_qk = jnp.concatenate([jnp.tile(q_gain, (1, attn_w // HEAD_DIM)),
                               jnp.tile(k_gain, (1, kv_w // HEAD_DIM))], axis=1)
    gw = fnet_w // FNET_GROUPS
    cg, sg = _dft_tables(gw)
    cs_chan = jnp.concatenate([cg, sg], axis=1).astype(BF)
    ct_l, st_l = (t.astype(BF) for t in _dft_tables(S))
    ct_c, st_c = (t.astype(BF) for t in _dft_tables(Tc))

    tm_l = min(1024, S)
    tm_c = B * Tc
    xl = x
    xc = ctx.reshape(1, B * Tc, D)
    streams = {"lat": (tm_l, mod_lat), "ctx": (tm_c, mod_ctx)}

    def ffn(xs, which, l, i, k_mod):
        tm, mods = streams[which]
        h = norm_mod(xs, g_norm[l, 2 * i], mods[l], k_mod, k_mod + 1)
        act = swiglu_in(h, w_ff_in_b, l, i, tm)
        return resid_out(act, w_ff_out_b, (l, i), xs, mods[l], k_mod + 2, 0.5, tm)

    for l in range(depth):
        last = l == depth - 1
        xl = ffn(xl, "lat", l, 0, 0)
        xc = ffn(xc, "ctx", l, 0, 0)

        hl = norm_mod(xl, g_norm[l, 1], mod_lat[l], 3, 4)
        hc = norm_mod(xc, g_norm[l, 1], mod_ctx[l], 3, 4)
        gain = gain_qk[l:l + 1]
        bias_g = b_gate[l].reshape(1, N_BRANCH * D)

        qk_c = proj_qk(hc, w_in_b, l, attn_w + kv_w, gain, cos_c, sin_c, B * Tc, tm_c)
        v_c = proj(hc, w_in_b, l, v0, kv_w, BF, tm_c).reshape(B, Tc, kv_w)
        xs_c = proj(hc, w_in_b, l, x0, lru_w, F32, tm_c).reshape(B, Tc, lru_w)
        qk_c = qk_c.reshape(B, Tc, -1)
        if last:
            gy_c = jnp.zeros((B, Tc, lru_w), F32)
        else:
            gy_c = proj(hc, w_in_b, l, y0, lru_w, F32, tm_c, act="gelu").reshape(B, Tc, lru_w)
        lru_args = (conv_w, conv_b, lru_wa_b, lru_ba, lru_wx_b, lru_bx, lru_lam, l)
        lru_c, h_fin = lru_mix(xs_c, gy_c, jnp.zeros((B, 2, lru_w), F32), *lru_args)

        qk_l = proj_qk(hl, w_in_b, l, attn_w + kv_w, gain, cos_l, sin_l, S, tm_l)
        v_l = proj(hl, w_in_b, l, v0, kv_w, BF, tm_l)
        xs_l = proj(hl, w_in_b, l, x0, lru_w, F32, tm_l)
        gy_l = proj(hl, w_in_b, l, y0, lru_w, F32, tm_l, act="gelu")
        f_l = proj(hl, w_in_b, l, f0, fnet_w, BF, tm_l)
        gate_l = proj(hl, w_in_b, l, g0, N_BRANCH * D, F32, tm_l, act="sigmoid_bias", bias=bias_g)

        attn_l = attention(qk_l, v_l, qk_c, v_c, attn_w, True)
        lru_l, _ = lru_mix(xs_l, gy_l, h_fin, *lru_args)
        zc, zs = chan_dft(f_l, cs_chan, tm_l)
        four_l = time_dft(zc, zs, ct_l, st_l, 1.0 / math.sqrt(S * gw))
        merged = gated_merge((attn_l, lru_l, four_l), w_branch_b, l, gate_l, tm_l)
        xl = resid_out(merged, w_out_b, (l,), xl, mod_lat[l], 5, 1.0, tm_l)
        xl = ffn(xl, "lat", l, 1, 6)

        if not last:
            f_c = proj(hc, w_in_b, l, f0, fnet_w, BF, tm_c)
            gate_c = proj(hc, w_in_b, l, g0, N_BRANCH * D, F32, tm_c, act="sigmoid_bias", bias=bias_g)
            attn_c = attention(qk_c, v_c, qk_c, v_c, attn_w, False)
            zc, zs = chan_dft(f_c, cs_chan, tm_c)
            four_c = time_dft(zc.reshape(B, Tc, fnet_w), zs.reshape(B, Tc, fnet_w), ct_c, st_c,
                              1.0 / math.sqrt(Tc * gw))
            merged_c = gated_merge((attn_c.reshape(1, B * Tc, attn_w), lru_c.reshape(1, B * Tc, lru_w),
                                    four_c.reshape(1, B * Tc, fnet_w)), w_branch_b, l, gate_c, tm_c)
            xc = resid_out(merged_c, w_out_b, (l,), xc, mod_ctx[l], 5, 1.0, tm_c)
            xc = ffn(xc, "ctx", l, 1, 6)

    return final_norm(xl, g_final)
```

```python
import functools
import math

import jax
import jax.numpy as jnp
from jax import lax
from jax.experimental import pallas as pl
from jax.experimental.pallas import tpu as pltpu

F32 = jnp.float32
BF = jnp.bfloat16

VMEM_LIMIT_BYTES = 56 * 1024 * 1024
LANES = 128
SUBLANES = 8

HEAD_DIM = 128
Q_GROUP = 4
GRID_W = 64
CONV_W = 4
CONV_LEFT = 2
LRU_C = 8.0
LRU_BLOCK = 128
FNET_GROUPS = 4
N_BRANCH = 3
N_MOD = 9
ROPE_THETA = 10000.0
EPS = 1e-6
CONV_PAD = 8


def _cparams(*sem):
    return pltpu.CompilerParams(dimension_semantics=sem, vmem_limit_bytes=VMEM_LIMIT_BYTES)


def _silu(a):
    return a * jax.nn.sigmoid(a)


def _dot(a, w):
    return jnp.dot(a, w.astype(BF), preferred_element_type=F32)


def _ada_kernel(c_ref, w_ref, b_ref, o_ref):
    s = _silu(c_ref[...]).astype(BF)
    o_ref[...] = _dot(s, w_ref[...]) + b_ref[...]


def ada_mod(c_rows, w_ada, b_ada, tn=1024):
    depth, d, n = w_ada.shape
    rows = c_rows.shape[0]
    return pl.pallas_call(
        _ada_kernel,
        out_shape=jax.ShapeDtypeStruct((depth, rows, n), F32),
        grid=(depth, n // tn),
        in_specs=[pl.BlockSpec((rows, d), lambda l, j: (0, 0)),
                  pl.BlockSpec((None, d, tn), lambda l, j: (l, 0, j)),
                  pl.BlockSpec((None, 1, tn), lambda l, j: (l, 0, j))],
        out_specs=pl.BlockSpec((None, rows, tn), lambda l, j: (l, 0, j)),
        compiler_params=_cparams("arbitrary", "arbitrary"),
        name="ada_mod",
    )(c_rows, w_ada, b_ada.reshape(depth, 1, n))


def _norm_mod_kernel(x_ref, g_ref, sh_ref, sc_ref, o_ref):
    x = x_ref[...]
    ms = jnp.mean(x * x, axis=-1, keepdims=True)
    y = x * lax.rsqrt(ms + EPS) * g_ref[...]
    o_ref[...] = (y * (1.0 + sc_ref[...]) + sh_ref[...]).astype(o_ref.dtype)


def norm_mod(x, g, mod, k_shift, k_scale, ts=256):
    G, S, D = x.shape
    ts = min(ts, S)
    return pl.pallas_call(
        _norm_mod_kernel,
        out_shape=jax.ShapeDtypeStruct((G, S, D), BF),
        grid=(G, S // ts),
        in_specs=[pl.BlockSpec((None, ts, D), lambda g_, i: (g_, i, 0)),
                  pl.BlockSpec((1, D), lambda g_, i: (0, 0)),
                  pl.BlockSpec((None, None, 1, D), lambda g_, i: (g_, k_shift, 0, 0)),
                  pl.BlockSpec((None, None, 1, D), lambda g_, i: (g_, k_scale, 0, 0))],
        out_specs=pl.BlockSpec((None, ts, D), lambda g_, i: (g_, i, 0)),
        compiler_params=_cparams("arbitrary", "arbitrary"),
        name="norm_mod",
    )(x, g.reshape(1, D), mod, mod)


def _final_norm_kernel(x_ref, g_ref, o_ref):
    x = x_ref[...]
    ms = jnp.mean(x * x, axis=-1, keepdims=True)
    o_ref[...] = x * lax.rsqrt(ms + EPS) * g_ref[...]


def final_norm(x, g, ts=256):
    G, S, D = x.shape
    return pl.pallas_call(
        _final_norm_kernel,
        out_shape=jax.ShapeDtypeStruct((G, S, D), F32),
        grid=(G, S // ts),
        in_specs=[pl.BlockSpec((None, ts, D), lambda g_, i: (g_, i, 0)),
                  pl.BlockSpec((1, D), lambda g_, i: (0, 0))],
        out_specs=pl.BlockSpec((None, ts, D), lambda g_, i: (g_, i, 0)),
        compiler_params=_cparams("arbitrary", "arbitrary"),
        name="final_norm",
    )(x, g.reshape(1, D))


def _swiglu_kernel(h_ref, wa_ref, wb_ref, o_ref):
    h = h_ref[...]
    a = _dot(h, wa_ref[...])
    b = _dot(h, wb_ref[...])
    o_ref[...] = (_silu(a) * b).astype(o_ref.dtype)


def swiglu_in(h, w, l, i, tm, tn=256):
    G, S, K = h.shape
    F = w.shape[-1] // 2
    nb = F // tn
    return pl.pallas_call(
        _swiglu_kernel,
        out_shape=jax.ShapeDtypeStruct((G, S, F), BF),
        grid=(G, S // tm, nb),
        in_specs=[pl.BlockSpec((None, tm, K), lambda g, m, n: (g, m, 0)),
                  pl.BlockSpec((None, None, K, tn), lambda g, m, n: (l, i, 0, n)),
                  pl.BlockSpec((None, None, K, tn), lambda g, m, n: (l, i, 0, n + nb))],
        out_specs=pl.BlockSpec((None, tm, tn), lambda g, m, n: (g, m, n)),
        compiler_params=_cparams("arbitrary", "arbitrary", "arbitrary"),
        name="swiglu_in",
    )(h, w, w)


def _resid_kernel(coef, a_ref, w_ref, x_ref, g_ref, o_ref):
    y = _dot(a_ref[...], w_ref[...])
    o_ref[...] = x_ref[...] + (coef * g_ref[...]) * y


def resid_out(a, w, w_idx, x, mod, k_gate, coef, tm):
    G, S, K = a.shape
    D = x.shape[-1]
    tn = 512 if K <= 4096 else 256
    nlead = len(w_idx)
    w_block = (None,) * nlead + (K, tn)
    return pl.pallas_call(
        functools.partial(_resid_kernel, coef),
        out_shape=jax.ShapeDtypeStruct((G, S, D), F32),
        grid=(G, S // tm, D // tn),
        in_specs=[pl.BlockSpec((None, tm, K), lambda g, m, n: (g, m, 0)),
                  pl.BlockSpec(w_block, lambda g, m, n: tuple(w_idx) + (0, n)),
                  pl.BlockSpec((None, tm, tn), lambda g, m, n: (g, m, n)),
                  pl.BlockSpec((None, None, 1, tn), lambda g, m, n: (g, k_gate, 0, n))],
        out_specs=pl.BlockSpec((None, tm, tn), lambda g, m, n: (g, m, n)),
        compiler_params=_cparams("arbitrary", "arbitrary", "arbitrary"),
        name="resid_out",
    )(a, w, x, mod)


def _proj_kernel(act, h_ref, w_ref, *rest):
    o_ref = rest[-1]
    acc = _dot(h_ref[...], w_ref[...])
    if act == "gelu":
        acc = jax.nn.gelu(acc)
    elif act == "sigmoid_bias":
        acc = jax.nn.sigmoid(acc + rest[0][...])
    o_ref[...] = acc.astype(o_ref.dtype)


def proj(h, w_in, l, col0, ncols, out_dtype, tm, act="none", bias=None, tn=512):
    G, S, K = h.shape
    cb = col0 // tn
    in_specs = [pl.BlockSpec((None, tm, K), lambda g, m, n: (g, m, 0)),
                pl.BlockSpec((None, K, tn), lambda g, m, n: (l, 0, cb + n))]
    args = [h, w_in]
    if bias is not None:
        in_specs.append(pl.BlockSpec((1, tn), lambda g, m, n: (0, n)))
        args.append(bias)
    return pl.pallas_call(
        functools.partial(_proj_kernel, act),
        out_shape=jax.ShapeDtypeStruct((G, S, ncols), out_dtype),
        grid=(G, S // tm, ncols // tn),
        in_specs=in_specs,
        out_specs=pl.BlockSpec((None, tm, tn), lambda g, m, n: (g, m, n)),
        compiler_params=_cparams("arbitrary", "arbitrary", "arbitrary"),
        name="proj_" + act,
    )(*args)


def _qk_kernel(h_ref, w_ref, gain_ref, cos_ref, sin_ref, o_ref):
    acc = _dot(h_ref[...], w_ref[...])
    cos = cos_ref[...]
    sin = sin_ref[...]
    lane = lax.broadcasted_iota(jnp.int32, (1, HEAD_DIM), 1)
    low_half = (lane % (HEAD_DIM // 2)) < (HEAD_DIM // 4)
    for hd in range(acc.shape[1] // HEAD_DIM):
        cols = slice(hd * HEAD_DIM, (hd + 1) * HEAD_DIM)
        v = acc[:, cols]
        ms = jnp.mean(v * v, axis=-1, keepdims=True)
        y = v * lax.rsqrt(ms + EPS) * gain_ref[:, cols]
        partner = jnp.where(low_half, pltpu.roll(y, HEAD_DIM - HEAD_DIM // 4, 1),
                            pltpu.roll(y, HEAD_DIM // 4, 1))
        o_ref[:, cols] = (y * cos + partner * sin).astype(o_ref.dtype)


def proj_qk(h, w_in, l, ncols, gain, cos, sin, rows_per_seq, tm, tn=512):
    G, S, K = h.shape
    seq_blocks = rows_per_seq // tm
    return pl.pallas_call(
        _qk_kernel,
        out_shape=jax.ShapeDtypeStruct((G, S, ncols), BF),
        grid=(G, S // tm, ncols // tn),
        in_specs=[pl.BlockSpec((None, tm, K), lambda g, m, n: (g, m, 0)),
                  pl.BlockSpec((None, K, tn), lambda g, m, n: (l, 0, n)),
                  pl.BlockSpec((1, tn), lambda g, m, n: (0, n)),
                  pl.BlockSpec((tm, HEAD_DIM), lambda g, m, n: (m % seq_blocks, 0)),
                  pl.BlockSpec((tm, HEAD_DIM), lambda g, m, n: (m % seq_blocks, 0))],
        out_specs=pl.BlockSpec((None, tm, tn), lambda g, m, n: (g, m, n)),
        compiler_params=_cparams("arbitrary", "arbitrary", "arbitrary"),
        name="proj_qk",
    )(h, w_in, gain, cos, sin)


def _attn_kernel(q_ref, kc_ref, vc_ref, *rest):
    o_ref = rest[-1]
    if len(rest) == 3:
        k = jnp.concatenate([kc_ref[...], rest[0][...]], axis=0)
        v = jnp.concatenate([vc_ref[...], rest[1][...]], axis=0)
    else:
        k, v = kc_ref[...], vc_ref[...]
    v_ones = jnp.concatenate([v, jnp.ones_like(v)], axis=1)
    for g in range(Q_GROUP):
        cols = slice(g * HEAD_DIM, (g + 1) * HEAD_DIM)
        s = lax.dot_general(q_ref[:, cols], k, (((1,), (1,)), ((), ())), preferred_element_type=F32)
        p = jnp.exp2(s - jnp.max(s, axis=-1, keepdims=True))
        acc = jnp.dot(p.astype(BF), v_ones, preferred_element_type=F32)
        o_ref[:, cols] = (acc[:, :HEAD_DIM] / acc[:, HEAD_DIM:]).astype(o_ref.dtype)


def attention(qk, v, qk_ctx, v_ctx, attn_width, with_latent_keys, tq=256):
    B, S, _ = qk.shape
    Tc = qk_ctx.shape[1]
    n_kv = v.shape[-1] // HEAD_DIM
    gw = Q_GROUP * HEAD_DIM
    kcol = attn_width // HEAD_DIM
    tq = min(tq, S)
    in_specs = [pl.BlockSpec((None, tq, gw), lambda b, h, i: (b, i, h)),
                pl.BlockSpec((None, Tc, HEAD_DIM), lambda b, h, i: (b, 0, kcol + h)),
                pl.BlockSpec((None, Tc, HEAD_DIM), lambda b, h, i: (b, 0, h))]
    args = [qk, qk_ctx, v_ctx]
    if with_latent_keys:
        in_specs += [pl.BlockSpec((None, S, HEAD_DIM), lambda b, h, i: (b, 0, kcol + h)),
                     pl.BlockSpec((None, S, HEAD_DIM), lambda b, h, i: (b, 0, h))]
        args += [qk, v]
    return pl.pallas_call(
        _attn_kernel,
        out_shape=jax.ShapeDtypeStruct((B, S, attn_width), BF),
        grid=(B, n_kv, S // tq),
        in_specs=in_specs,
        out_specs=pl.BlockSpec((None, tq, gw), lambda b, h, i: (b, i, h)),
        compiler_params=_cparams("arbitrary", "arbitrary", "arbitrary"),
        name="attention",
    )(*args)


def _neg_expm1(y):
    poly = y * (1.0 / 720.0) + (1.0 / 120.0)
    for coef in (1.0 / 24.0, 1.0 / 6.0, 0.5, 1.0):
        poly = poly * y + coef
    return jnp.where(y > -0.0625, -(poly * y), 1.0 - jnp.exp(y))


def _tile_scan(a, b, reverse):
    row = lax.broadcasted_iota(jnp.int32, a.shape, 0)
    for d in (1, 2, 4):
        shift = SUBLANES - d if reverse else d
        valid = (row < SUBLANES - d) if reverse else (row >= d)
        a_prev = jnp.where(valid, pltpu.roll(a, shift, 0), 1.0)
        b_prev = jnp.where(valid, pltpu.roll(b, shift, 0), 0.0)
        b = a * b_prev + b
        a = a * a_prev
    return a, b


def _lru_kernel(S, tchunk, unroll, x_ref, gy_ref, h0_ref, cw_ref, cb_ref, wa_ref, ba_ref, wx_ref, bx_ref,
                lam_ref, o_ref, hfin_ref, xs_ref, a_ref, b_ref, h_ref):
    zeros = jnp.zeros((CONV_PAD, LRU_BLOCK), F32)
    xs_ref[0:CONV_PAD, :] = zeros
    xs_ref[CONV_PAD + S:2 * CONV_PAD + S, :] = zeros
    xs_ref[CONV_PAD:CONV_PAD + S, :] = x_ref[...]

    neg_lam = -lam_ref[...]
    softplus = jnp.maximum(neg_lam, 0.0) + jnp.log1p(jnp.exp(-jnp.abs(neg_lam)))

    for c in range(S // tchunk):
        t0 = c * tchunk
        xc = cb_ref[...]
        for j in range(CONV_W):
            xc = xc + xs_ref[pl.ds(CONV_PAD + t0 + j - CONV_LEFT, tchunk), :] * cw_ref[j:j + 1, :]
        xb = xc.astype(BF)
        for d in range(2):
            r = jax.nn.sigmoid(jnp.dot(xb, wa_ref[d], preferred_element_type=F32) + ba_ref[d:d + 1, :])
            i = jax.nn.sigmoid(jnp.dot(xb, wx_ref[d], preferred_element_type=F32) + bx_ref[d:d + 1, :])
            log_a = -LRU_C * r * softplus[d:d + 1, :]
            a_ref[d, pl.ds(t0, tchunk), :] = jnp.exp(log_a)
            b_ref[d, pl.ds(t0, tchunk), :] = jnp.sqrt(_neg_expm1(2.0 * log_a)) * (i * xc)

    n_tiles = S // SUBLANES

    def body(i, carry):
        hf, hr = carry
        tf = pl.multiple_of(i * SUBLANES, SUBLANES)
        af, bf = _tile_scan(a_ref[0, pl.ds(tf, SUBLANES), :], b_ref[0, pl.ds(tf, SUBLANES), :], False)
        h = bf + af * hf
        h_ref[0, pl.ds(tf, SUBLANES), :] = h
        hf = h[SUBLANES - 1:SUBLANES, :]
        tr = pl.multiple_of((n_tiles - 1 - i) * SUBLANES, SUBLANES)
        ar, br = _tile_scan(a_ref[1, pl.ds(tr, SUBLANES), :], b_ref[1, pl.ds(tr, SUBLANES), :], True)
        h = br + ar * hr
        h_ref[1, pl.ds(tr, SUBLANES), :] = h
        hr = h[0:1, :]
        return hf, hr

    hf, hr = lax.fori_loop(0, n_tiles, body, (h0_ref[0:1, :], h0_ref[1:2, :]), unroll=unroll)
    hfin_ref[0:1, :] = hf
    hfin_ref[1:2, :] = hr
    o_ref[...] = ((h_ref[0] + h_ref[1]) * gy_ref[...]).astype(o_ref.dtype)


def lru_mix(xseg, gy, h0, conv_w, conv_b, wa, ba, wx, bx, lam, l, tchunk=512, unroll=4):
    B, S, W = xseg.shape
    tchunk = min(tchunk, S)
    nblk = W // LRU_BLOCK
    seq = pl.BlockSpec((None, S, LRU_BLOCK), lambda b, n: (b, 0, n))
    st = pl.BlockSpec((None, 2, LRU_BLOCK), lambda b, n: (b, 0, n))
    vec2 = pl.BlockSpec((None, 2, LRU_BLOCK), lambda b, n: (l, 0, n))
    wblk = pl.BlockSpec((None, 2, None, LRU_BLOCK, LRU_BLOCK), lambda b, n: (l, 0, n, 0, 0))
    return pl.pallas_call(
        functools.partial(_lru_kernel, S, tchunk, unroll),
        out_shape=(jax.ShapeDtypeStruct((B, S, W), BF), jax.ShapeDtypeStruct((B, 2, W), F32)),
        grid=(B, nblk),
        in_specs=[seq, seq, st,
                  pl.BlockSpec((None, CONV_W, LRU_BLOCK), lambda b, n: (l, 0, n)),
                  pl.BlockSpec((None, 1, LRU_BLOCK), lambda b, n: (l, 0, n)),
                  wblk, vec2, wblk, vec2, vec2],
        out_specs=(seq, st),
        scratch_shapes=[pltpu.VMEM((S + 2 * CONV_PAD, LRU_BLOCK), F32),
                        pltpu.VMEM((2, S, LRU_BLOCK), F32),
                        pltpu.VMEM((2, S, LRU_BLOCK), F32),
                        pltpu.VMEM((2, S, LRU_BLOCK), F32)],
        compiler_params=_cparams("arbitrary", "arbitrary"),
        name="lru_mix",
    )(xseg, gy, h0, conv_w, conv_b.reshape(conv_b.shape[0], 1, W), wa, ba, wx, bx, lam)


def _dft_tables(n):
    k = jnp.arange(n, dtype=jnp.int32)
    ang = ((k[:, None] * k[None, :]) % n).astype(F32) * (2.0 * math.pi / n)
    return jnp.cos(ang), jnp.sin(ang)


def _chan_dft_kernel(f_ref, cs_ref, zc_ref, zs_ref):
    w = zc_ref.shape[-1]
    z = jnp.dot(f_ref[...], cs_ref[...], preferred_element_type=F32)
    zc_ref[...] = z[:, :w].astype(zc_ref.dtype)
    zs_ref[...] = z[:, w:].astype(zs_ref.dtype)


def chan_dft(f, cs, tm, out_dtype):
    G, S, W = f.shape
    gw = W // FNET_GROUPS
    blk = pl.BlockSpec((None, tm, gw), lambda g, m, n: (g, m, n))
    return pl.pallas_call(
        _chan_dft_kernel,
        out_shape=(jax.ShapeDtypeStruct((G, S, W), out_dtype),) * 2,
        grid=(G, S // tm, FNET_GROUPS),
        in_specs=[blk, pl.BlockSpec((gw, 2 * gw), lambda g, m, n: (0, 0))],
        out_specs=(blk, blk),
        compiler_params=_cparams("arbitrary", "arbitrary", "arbitrary"),
        name="chan_dft",
    )(f, cs)


def _time_dft_kernel(scale, c_ref, s_ref, zc_ref, zs_ref, o_ref):
    re = jnp.dot(c_ref[...], zc_ref[...], preferred_element_type=F32)
    re = re - jnp.dot(s_ref[...], zs_ref[...], preferred_element_type=F32)
    o_ref[...] = (re * scale).astype(o_ref.dtype)


def time_dft_dense(zc, zs, scale, tm=512, tn=512):
    B, T, W = zc.shape
    tm = min(tm, T)
    ct, st = (t.astype(BF) for t in _dft_tables(T))
    a_spec = pl.BlockSpec((tm, T), lambda b, n, m: (m, 0))
    z_spec = pl.BlockSpec((None, T, tn), lambda b, n, m: (b, 0, n))
    return pl.pallas_call(
        functools.partial(_time_dft_kernel, scale),
        out_shape=jax.ShapeDtypeStruct((B, T, W), BF),
        grid=(B, W // tn, T // tm),
        in_specs=[a_spec, a_spec, z_spec, z_spec],
        out_specs=pl.BlockSpec((None, tm, tn), lambda b, n, m: (b, m, n)),
        compiler_params=_cparams("arbitrary", "arbitrary", "arbitrary"),
        name="time_dft",
    )(ct, st, zc, zs)


def _dft_stage1_kernel(zc_ref, zs_ref, rot_ref, twc_ref, tws_ref, oc_ref, os_ref):
    n1 = zc_ref.shape[0]
    tn = zc_ref.shape[-1]
    rot = rot_ref[...]
    for j in range(zc_ref.shape[1]):
        z = jnp.concatenate([zc_ref[:, j, :], zs_ref[:, j, :]], axis=0).astype(BF)
        res = jnp.dot(rot, z, preferred_element_type=F32)
        ac, as_ = res[:n1], res[n1:]
        twc = jnp.tile(twc_ref[j], (1, tn // LANES))
        tws = jnp.tile(tws_ref[j], (1, tn // LANES))
        oc_ref[j] = (ac * twc - as_ * tws).astype(oc_ref.dtype)
        os_ref[j] = (as_ * twc + ac * tws).astype(os_ref.dtype)


def dft_stage1(zc, zs, n1, n2, tn=512):
    B, T, W = zc.shape
    c, s = _dft_tables(n1)
    rot = jnp.concatenate([jnp.concatenate([c, -s], axis=1), jnp.concatenate([s, c], axis=1)], axis=0).astype(BF)
    cc = jnp.arange(n1, dtype=jnp.int32)
    bb = jnp.arange(n2, dtype=jnp.int32)
    ang = (bb[:, None] * cc[None, :]).astype(F32) * (2.0 * math.pi / T)
    twc = jnp.broadcast_to(jnp.cos(ang)[:, :, None], (n2, n1, LANES))
    tws = jnp.broadcast_to(jnp.sin(ang)[:, :, None], (n2, n1, LANES))
    z_spec = pl.BlockSpec((None, n1, SUBLANES, tn), lambda b, i, n: (b, 0, i, n))
    tw_spec = pl.BlockSpec((SUBLANES, n1, LANES), lambda b, i, n: (i, 0, 0))
    o_spec = pl.BlockSpec((None, SUBLANES, n1, tn), lambda b, i, n: (b, i, 0, n))
    return pl.pallas_call(
        _dft_stage1_kernel,
        out_shape=(jax.ShapeDtypeStruct((B, n2, n1, W), BF),) * 2,
        grid=(B, n2 // SUBLANES, W // tn),
        in_specs=[z_spec, z_spec, pl.BlockSpec((2 * n1, 2 * n1), lambda b, i, n: (0, 0)), tw_spec, tw_spec],
        out_specs=(o_spec, o_spec),
        compiler_params=_cparams("arbitrary", "arbitrary", "arbitrary"),
        name="dft_stage1",
    )(zc.reshape(B, n1, n2, W), zs.reshape(B, n1, n2, W), rot, twc, tws)


def _dft_stage2_kernel(scale, ac_ref, as_ref, cs_ref, o_ref):
    a = jnp.concatenate([ac_ref[...], as_ref[...]], axis=0)
    o_ref[...] = (jnp.dot(cs_ref[...], a, preferred_element_type=F32) * scale).astype(o_ref.dtype)


def dft_stage2(ac, as_, scale, tn=8192):
    B, n2, n1, W = ac.shape
    c, s = _dft_tables(n2)
    cs = jnp.concatenate([c, -s], axis=1).astype(BF)
    N = n1 * W
    tn = min(tn, N)
    a_spec = pl.BlockSpec((None, n2, tn), lambda b, n: (b, 0, n))
    out = pl.pallas_call(
        functools.partial(_dft_stage2_kernel, scale),
        out_shape=jax.ShapeDtypeStruct((B, n2, N), BF),
        grid=(B, N // tn),
        in_specs=[a_spec, a_spec, pl.BlockSpec((n2, 2 * n2), lambda b, n: (0, 0))],
        out_specs=a_spec,
        compiler_params=_cparams("arbitrary", "arbitrary"),
        name="dft_stage2",
    )(ac.reshape(B, n2, N), as_.reshape(B, n2, N), cs)
    return out.reshape(B, n2 * n1, W)


def _two_stage_dft_ok(T):
    n = math.isqrt(T)
    return n * n == T and n % SUBLANES == 0


def fourier_branch(f, cs_chan, tm, B, T, scale):
    W = f.shape[-1]
    if _two_stage_dft_ok(T):
        n = math.isqrt(T)
        zc, zs = chan_dft(f, cs_chan, tm, F32)
        ac, as_ = dft_stage1(zc.reshape(B, T, W), zs.reshape(B, T, W), n, n)
        return dft_stage2(ac, as_, scale)
    zc, zs = chan_dft(f, cs_chan, tm, BF)
    return time_dft_dense(zc.reshape(B, T, W), zs.reshape(B, T, W), scale)


def _merge_kernel(b0_ref, b1_ref, b2_ref, w_ref, g0_ref, g1_ref, g2_ref, o_ref):
    acc = g0_ref[...] * _dot(b0_ref[...], w_ref[0])
    acc = acc + g1_ref[...] * _dot(b1_ref[...], w_ref[1])
    acc = acc + g2_ref[...] * _dot(b2_ref[...], w_ref[2])
    o_ref[...] = acc.astype(o_ref.dtype)


def gated_merge(branches, w_branch, l, gates, tm, tn=256):
    G, S, K = branches[0].shape
    D = w_branch.shape[-1]
    nb = D // tn
    b_spec = pl.BlockSpec((None, tm, K), lambda g, m, n: (g, m, 0))

    def g_spec(j):
        return pl.BlockSpec((None, tm, tn), lambda g, m, n: (g, m, j * nb + n))

    return pl.pallas_call(
        _merge_kernel,
        out_shape=jax.ShapeDtypeStruct((G, S, D), BF),
        grid=(G, S // tm, nb),
        in_specs=[b_spec, b_spec, b_spec,
                  pl.BlockSpec((None, N_BRANCH, K, tn), lambda g, m, n: (l, 0, 0, n)),
                  g_spec(0), g_spec(1), g_spec(2)],
        out_specs=pl.BlockSpec((None, tm, tn), lambda g, m, n: (g, m, n)),
        compiler_params=_cparams("arbitrary", "arbitrary", "arbitrary"),
        name="gated_merge",
    )(*branches, w_branch, gates, gates, gates)


def _rope_tables(n_tokens):
    rows = n_tokens // GRID_W
    row = jnp.repeat(jnp.arange(rows, dtype=F32), GRID_W)
    col = jnp.tile(jnp.arange(GRID_W, dtype=F32), rows)
    axis_dim = HEAD_DIM // 2
    inv_freq = ROPE_THETA ** (-jnp.arange(0, axis_dim, 2, dtype=F32) / axis_dim)
    ar, ac = row[:, None] * inv_freq, col[:, None] * inv_freq
    cos = jnp.concatenate([jnp.cos(ar), jnp.cos(ar), jnp.cos(ac), jnp.cos(ac)], axis=-1)
    sin = jnp.concatenate([-jnp.sin(ar), jnp.sin(ar), -jnp.sin(ac), jnp.sin(ac)], axis=-1)
    return cos, sin


def kernel(x, c, ctx, c_ctx, w_ada, b_ada, g_norm, w_ff_in, w_ff_out, w_in, b_gate, q_gain, k_gain,
           conv_w, conv_b, lru_wa, lru_ba, lru_wx, lru_bx, lru_lam, w_branch, w_out, g_final):
    B, S, D = x.shape
    Tc = ctx.shape[1]
    depth = w_ada.shape[0]
    lru_w = conv_w.shape[-1]
    fnet_w = D // 2
    attn_w = (D // 256) * HEAD_DIM
    kv_w = attn_w // Q_GROUP
    v0 = attn_w + kv_w
    x0 = v0 + kv_w
    y0 = x0 + lru_w
    f0 = y0 + lru_w
    g0 = f0 + fnet_w

    lru_wa_b = lru_wa.astype(BF)
    lru_wx_b = lru_wx.astype(BF)

    n_rows = SUBLANES
    c_rows = jnp.concatenate([c, c_ctx[None, :], jnp.zeros((n_rows - B - 1, D), F32)], axis=0)
    mod = ada_mod(c_rows, w_ada, b_ada).reshape(depth, n_rows, N_MOD, 1, D)
    mod_lat = mod[:, :B]
    mod_ctx = mod[:, B:B + 1]

    cos_l, sin_l = _rope_tables(S)
    cos_c, sin_c = jnp.ones((B * Tc, HEAD_DIM), F32), jnp.zeros((B * Tc, HEAD_DIM), F32)
    q_fold = HEAD_DIM ** -0.5 * math.log2(math.e)
    gain_qk = jnp.concatenate([jnp.tile(q_gain * q_fold, (1, attn_w // HEAD_DIM)),
                               jnp.tile(k_gain, (1, kv_w // HEAD_DIM))], axis=1)
    gw = fnet_w // FNET_GROUPS
    cg, sg = _dft_tables(gw)
    cs_chan = jnp.concatenate([cg, sg], axis=1).astype(BF)

    tm_l = min(1024, S)
    tm_c = B * Tc
    xl = x
    xc = ctx.reshape(1, B * Tc, D)
    streams = {"lat": (tm_l, mod_lat), "ctx": (tm_c, mod_ctx)}

    def ffn(xs, which, l, i, k_mod):
        tm, mods = streams[which]
        h = norm_mod(xs, g_norm[l, 2 * i], mods[l], k_mod, k_mod + 1)
        act = swiglu_in(h, w_ff_in, l, i, tm)
        return resid_out(act, w_ff_out, (l, i), xs, mods[l], k_mod + 2, 0.5, tm)

    for l in range(depth):
        last = l == depth - 1
        xl = ffn(xl, "lat", l, 0, 0)
        xc = ffn(xc, "ctx", l, 0, 0)

        hl = norm_mod(xl, g_norm[l, 1], mod_lat[l], 3, 4)
        hc = norm_mod(xc, g_norm[l, 1], mod_ctx[l], 3, 4)
        gain = gain_qk[l:l + 1]
        bias_g = b_gate[l].reshape(1, N_BRANCH * D)

        qk_c = proj_qk(hc, w_in, l, attn_w + kv_w, gain, cos_c, sin_c, B * Tc, tm_c).reshape(B, Tc, -1)
        v_c = proj(hc, w_in, l, v0, kv_w, BF, tm_c).reshape(B, Tc, kv_w)
        xs_c = proj(hc, w_in, l, x0, lru_w, F32, tm_c).reshape(B, Tc, lru_w)
        if last:
            gy_c = jnp.zeros((B, Tc, lru_w), F32)
        else:
            gy_c = proj(hc, w_in, l, y0, lru_w, F32, tm_c, act="gelu").reshape(B, Tc, lru_w)
        lru_args = (conv_w, conv_b, lru_wa_b, lru_ba, lru_wx_b, lru_bx, lru_lam, l)
        lru_c, h_fin = lru_mix(xs_c, gy_c, jnp.zeros((B, 2, lru_w), F32), *lru_args)

        qk_l = proj_qk(hl, w_in, l, attn_w + kv_w, gain, cos_l, sin_l, S, tm_l)
        v_l = proj(hl, w_in, l, v0, kv_w, BF, tm_l)
        xs_l = proj(hl, w_in, l, x0, lru_w, F32, tm_l)
        gy_l = proj(hl, w_in, l, y0, lru_w, F32, tm_l, act="gelu")
        f_l = proj(hl, w_in, l, f0, fnet_w, BF, tm_l)
        gate_l = proj(hl, w_in, l, g0, N_BRANCH * D, F32, tm_l, act="sigmoid_bias", bias=bias_g)

        attn_l = attention(qk_l, v_l, qk_c, v_c, attn_w, True)
        lru_l, _ = lru_mix(xs_l, gy_l, h_fin, *lru_args)
        four_l = fourier_branch(f_l, cs_chan, tm_l, B, S, 1.0 / math.sqrt(S * gw))
        merged = gated_merge((attn_l, lru_l, four_l), w_branch, l, gate_l, tm_l)
        xl = resid_out(merged, w_out, (l,), xl, mod_lat[l], 5, 1.0, tm_l)
        xl = ffn(xl, "lat", l, 1, 6)

        if not last:
            f_c = proj(hc, w_in, l, f0, fnet_w, BF, tm_c)
            gate_c = proj(hc, w_in, l, g0, N_BRANCH * D, F32, tm_c, act="sigmoid_bias", bias=bias_g)
            attn_c = attention(qk_c, v_c, qk_c, v_c, attn_w, False)
            four_c = fourier_branch(f_c, cs_chan, tm_c, B, Tc, 1.0 / math.sqrt(Tc * gw))
            merged_c = gated_merge((attn_c.reshape(1, B * Tc, attn_w), lru_c.reshape(1, B * Tc, lru_w),
                                    four_c.reshape(1, B * Tc, fnet_w)), w_branch, l, gate_c, tm_c)
            xc = resid_out(merged_c, w_out, (l,), xc, mod_ctx[l], 5, 1.0, tm_c)
            xc = ffn(xc, "ctx", l, 1, 6)

    return final_norm(xl, g_final)
```

```python
import functools
import math

import jax
import jax.numpy as jnp
from jax import lax
from jax.experimental import pallas as pl
from jax.experimental.pallas import tpu as pltpu

F32 = jnp.float32
BF = jnp.bfloat16

VMEM_LIMIT_BYTES = 56 * 1024 * 1024
LANES = 128
SUBLANES = 8

HEAD_DIM = 128
Q_GROUP = 4
GRID_W = 64
CONV_W = 4
CONV_LEFT = 2
LRU_C = 8.0
LRU_BLOCK = 128
FNET_GROUPS = 4
N_BRANCH = 3
N_MOD = 9
ROPE_THETA = 10000.0
EPS = 1e-6
CONV_PAD = 8


def _cparams(*sem):
    return pltpu.CompilerParams(dimension_semantics=sem, vmem_limit_bytes=VMEM_LIMIT_BYTES)


def _silu(a):
    return a * jax.nn.sigmoid(a)


def _dot(a, w):
    return jnp.dot(a, w.astype(BF), preferred_element_type=F32)


def _ada_kernel(c_ref, w_ref, b_ref, o_ref):
    s = _silu(c_ref[...]).astype(BF)
    o_ref[...] = _dot(s, w_ref[...]) + b_ref[...]


def ada_mod(c_rows, w_ada, b_ada, tn=1024):
    depth, d, n = w_ada.shape
    rows = c_rows.shape[0]
    return pl.pallas_call(
        _ada_kernel,
        out_shape=jax.ShapeDtypeStruct((depth, rows, n), F32),
        grid=(depth, n // tn),
        in_specs=[pl.BlockSpec((rows, d), lambda l, j: (0, 0)),
                  pl.BlockSpec((None, d, tn), lambda l, j: (l, 0, j)),
                  pl.BlockSpec((None, 1, tn), lambda l, j: (l, 0, j))],
        out_specs=pl.BlockSpec((None, rows, tn), lambda l, j: (l, 0, j)),
        compiler_params=_cparams("arbitrary", "arbitrary"),
        name="ada_mod",
    )(c_rows, w_ada, b_ada.reshape(depth, 1, n))


def _norm_mod_kernel(x_ref, g_ref, sh_ref, sc_ref, o_ref):
    x = x_ref[...]
    ms = jnp.mean(x * x, axis=-1, keepdims=True)
    y = x * lax.rsqrt(ms + EPS) * g_ref[...]
    o_ref[...] = (y * (1.0 + sc_ref[...]) + sh_ref[...]).astype(o_ref.dtype)


def norm_mod(x, g, mod, k_shift, k_scale, ts=256):
    G, S, D = x.shape
    ts = min(ts, S)
    return pl.pallas_call(
        _norm_mod_kernel,
        out_shape=jax.ShapeDtypeStruct((G, S, D), BF),
        grid=(G, S // ts),
        in_specs=[pl.BlockSpec((None, ts, D), lambda g_, i: (g_, i, 0)),
                  pl.BlockSpec((1, D), lambda g_, i: (0, 0)),
                  pl.BlockSpec((None, None, 1, D), lambda g_, i: (g_, k_shift, 0, 0)),
                  pl.BlockSpec((None, None, 1, D), lambda g_, i: (g_, k_scale, 0, 0))],
        out_specs=pl.BlockSpec((None, ts, D), lambda g_, i: (g_, i, 0)),
        compiler_params=_cparams("arbitrary", "arbitrary"),
        name="norm_mod",
    )(x, g.reshape(1, D), mod, mod)


def _final_norm_kernel(x_ref, g_ref, o_ref):
    x = x_ref[...]
    ms = jnp.mean(x * x, axis=-1, keepdims=True)
    o_ref[...] = x * lax.rsqrt(ms + EPS) * g_ref[...]


def final_norm(x, g, ts=256):
    G, S, D = x.shape
    return pl.pallas_call(
        _final_norm_kernel,
        out_shape=jax.ShapeDtypeStruct((G, S, D), F32),
        grid=(G, S // ts),
        in_specs=[pl.BlockSpec((None, ts, D), lambda g_, i: (g_, i, 0)),
                  pl.BlockSpec((1, D), lambda g_, i: (0, 0))],
        out_specs=pl.BlockSpec((None, ts, D), lambda g_, i: (g_, i, 0)),
        compiler_params=_cparams("arbitrary", "arbitrary"),
        name="final_norm",
    )(x, g.reshape(1, D))


def _swiglu_kernel(h_ref, wa_ref, wb_ref, o_ref):
    h = h_ref[...]
    a = _dot(h, wa_ref[...])
    b = _dot(h, wb_ref[...])
    o_ref[...] = (_silu(a) * b).astype(o_ref.dtype)


def swiglu_in(h, w, l, i, tm, tn=256):
    G, S, K = h.shape
    F = w.shape[-1] // 2
    nb = F // tn
    return pl.pallas_call(
        _swiglu_kernel,
        out_shape=jax.ShapeDtypeStruct((G, S, F), BF),
        grid=(G, S // tm, nb),
        in_specs=[pl.BlockSpec((None, tm, K), lambda g, m, n: (g, m, 0)),
                  pl.BlockSpec((None, None, K, tn), lambda g, m, n: (l, i, 0, n)),
                  pl.BlockSpec((None, None, K, tn), lambda g, m, n: (l, i, 0, n + nb))],
        out_specs=pl.BlockSpec((None, tm, tn), lambda g, m, n: (g, m, n)),
        compiler_params=_cparams("arbitrary", "arbitrary", "arbitrary"),
        name="swiglu_in",
    )(h, w, w)


def _resid_kernel(coef, a_ref, w_ref, x_ref, g_ref, o_ref):
    y = _dot(a_ref[...], w_ref[...])
    o_ref[...] = x_ref[...] + (coef * g_ref[...]) * y


def resid_out(a, w, w_idx, x, mod, k_gate, coef, tm):
    G, S, K = a.shape
    D = x.shape[-1]
    tn = 512 if K <= 4096 else 256
    nlead = len(w_idx)
    w_block = (None,) * nlead + (K, tn)
    return pl.pallas_call(
        functools.partial(_resid_kernel, coef),
        out_shape=jax.ShapeDtypeStruct((G, S, D), F32),
        grid=(G, S // tm, D // tn),
        in_specs=[pl.BlockSpec((None, tm, K), lambda g, m, n: (g, m, 0)),
                  pl.BlockSpec(w_block, lambda g, m, n: tuple(w_idx) + (0, n)),
                  pl.BlockSpec((None, tm, tn), lambda g, m, n: (g, m, n)),
                  pl.BlockSpec((None, None, 1, tn), lambda g, m, n: (g, k_gate, 0, n))],
        out_specs=pl.BlockSpec((None, tm, tn), lambda g, m, n: (g, m, n)),
        compiler_params=_cparams("arbitrary", "arbitrary", "arbitrary"),
        name="resid_out",
    )(a, w, x, mod)


def _proj_kernel(act, h_ref, w_ref, *rest):
    o_ref = rest[-1]
    acc = _dot(h_ref[...], w_ref[...])
    if act == "gelu":
        acc = jax.nn.gelu(acc)
    elif act == "sigmoid_bias":
        acc = jax.nn.sigmoid(acc + rest[0][...])
    o_ref[...] = acc.astype(o_ref.dtype)


def proj(h, w_in, l, col0, ncols, out_dtype, tm, act="none", bias=None, tn=512):
    G, S, K = h.shape
    cb = col0 // tn
    in_specs = [pl.BlockSpec((None, tm, K), lambda g, m, n: (g, m, 0)),
                pl.BlockSpec((None, K, tn), lambda g, m, n: (l, 0, cb + n))]
    args = [h, w_in]
    if bias is not None:
        in_specs.append(pl.BlockSpec((1, tn), lambda g, m, n: (0, n)))
        args.append(bias)
    return pl.pallas_call(
        functools.partial(_proj_kernel, act),
        out_shape=jax.ShapeDtypeStruct((G, S, ncols), out_dtype),
        grid=(G, S // tm, ncols // tn),
        in_specs=in_specs,
        out_specs=pl.BlockSpec((None, tm, tn), lambda g, m, n: (g, m, n)),
        compiler_params=_cparams("arbitrary", "arbitrary", "arbitrary"),
        name="proj_" + act,
    )(*args)


def _qk_kernel(h_ref, w_ref, gain_ref, cos_ref, sin_ref, swap_ref, o_ref):
    acc = _dot(h_ref[...], w_ref[...])
    cos = cos_ref[...]
    sin = sin_ref[...]
    swap = swap_ref[...]
    for hd in range(acc.shape[1] // HEAD_DIM):
        cols = slice(hd * HEAD_DIM, (hd + 1) * HEAD_DIM)
        v = acc[:, cols]
        ms = jnp.mean(v * v, axis=-1, keepdims=True)
        y = v * lax.rsqrt(ms + EPS) * gain_ref[:, cols]
        hi = y.astype(BF)
        lo = (y - hi.astype(F32)).astype(BF)
        partner = (jnp.dot(hi, swap, preferred_element_type=F32)
                   + jnp.dot(lo, swap, preferred_element_type=F32))
        o_ref[:, cols] = (y * cos + partner * sin).astype(o_ref.dtype)


def proj_qk(h, w_in, l, ncols, gain, cos, sin, rows_per_seq, tm, tn=512):
    G, S, K = h.shape
    seq_blocks = rows_per_seq // tm
    lane = jnp.arange(HEAD_DIM)
    quarter = HEAD_DIM // 4
    partner_lane = jnp.where((lane % (2 * quarter)) < quarter, lane + quarter, lane - quarter)
    swap = (lane[:, None] == partner_lane[None, :]).astype(BF)
    return pl.pallas_call(
        _qk_kernel,
        out_shape=jax.ShapeDtypeStruct((G, S, ncols), BF),
        grid=(G, S // tm, ncols // tn),
        in_specs=[pl.BlockSpec((None, tm, K), lambda g, m, n: (g, m, 0)),
                  pl.BlockSpec((None, K, tn), lambda g, m, n: (l, 0, n)),
                  pl.BlockSpec((1, tn), lambda g, m, n: (0, n)),
                  pl.BlockSpec((tm, HEAD_DIM), lambda g, m, n: (m % seq_blocks, 0)),
                  pl.BlockSpec((tm, HEAD_DIM), lambda g, m, n: (m % seq_blocks, 0)),
                  pl.BlockSpec((HEAD_DIM, HEAD_DIM), lambda g, m, n: (0, 0))],
        out_specs=pl.BlockSpec((None, tm, tn), lambda g, m, n: (g, m, n)),
        compiler_params=_cparams("arbitrary", "arbitrary", "arbitrary"),
        name="proj_qk",
    )(h, w_in, gain, cos, sin, swap)


def _attn_kernel(q_ref, kc_ref, vc_ref, *rest):
    o_ref = rest[-1]
    if len(rest) == 3:
        k = jnp.concatenate([kc_ref[...], rest[0][...]], axis=0)
        v = jnp.concatenate([vc_ref[...], rest[1][...]], axis=0)
    else:
        k, v = kc_ref[...], vc_ref[...]
    v_ones = jnp.concatenate([v, jnp.ones_like(v)], axis=1)
    for g in range(Q_GROUP):
        cols = slice(g * HEAD_DIM, (g + 1) * HEAD_DIM)
        s = lax.dot_general(q_ref[:, cols], k, (((1,), (1,)), ((), ())), preferred_element_type=F32)
        p = jnp.exp2(s - jnp.max(s, axis=-1, keepdims=True))
        acc = jnp.dot(p.astype(BF), v_ones, preferred_element_type=F32)
        o_ref[:, cols] = (acc[:, :HEAD_DIM] / acc[:, HEAD_DIM:]).astype(o_ref.dtype)


def attention(qk, v, qk_ctx, v_ctx, attn_width, with_latent_keys, tq=256):
    B, S, _ = qk.shape
    Tc = qk_ctx.shape[1]
    n_kv = v.shape[-1] // HEAD_DIM
    gw = Q_GROUP * HEAD_DIM
    kcol = attn_width // HEAD_DIM
    tq = min(tq, S)
    in_specs = [pl.BlockSpec((None, tq, gw), lambda b, h, i: (b, i, h)),
                pl.BlockSpec((None, Tc, HEAD_DIM), lambda b, h, i: (b, 0, kcol + h)),
                pl.BlockSpec((None, Tc, HEAD_DIM), lambda b, h, i: (b, 0, h))]
    args = [qk, qk_ctx, v_ctx]
    if with_latent_keys:
        in_specs += [pl.BlockSpec((None, S, HEAD_DIM), lambda b, h, i: (b, 0, kcol + h)),
                     pl.BlockSpec((None, S, HEAD_DIM), lambda b, h, i: (b, 0, h))]
        args += [qk, v]
    return pl.pallas_call(
        _attn_kernel,
        out_shape=jax.ShapeDtypeStruct((B, S, attn_width), BF),
        grid=(B, n_kv, S // tq),
        in_specs=in_specs,
        out_specs=pl.BlockSpec((None, tq, gw), lambda b, h, i: (b, i, h)),
        compiler_params=_cparams("arbitrary", "arbitrary", "arbitrary"),
        name="attention",
    )(*args)


def _neg_expm1(y):
    poly = y * (1.0 / 720.0) + (1.0 / 120.0)
    for coef in (1.0 / 24.0, 1.0 / 6.0, 0.5, 1.0):
        poly = poly * y + coef
    return jnp.where(y > -0.0625, -(poly * y), 1.0 - jnp.exp(y))


def _tile_scan(a, b, reverse):
    row = lax.broadcasted_iota(jnp.int32, a.shape, 0)
    for d in (1, 2, 4):
        shift = SUBLANES - d if reverse else d
        valid = (row < SUBLANES - d) if reverse else (row >= d)
        a_prev = jnp.where(valid, pltpu.roll(a, shift, 0), 1.0)
        b_prev = jnp.where(valid, pltpu.roll(b, shift, 0), 0.0)
        b = a * b_prev + b
        a = a * a_prev
    return a, b


def _lru_kernel(S, tchunk, unroll, x_ref, gy_ref, h0_ref, cw_ref, cb_ref, wa_ref, ba_ref, wx_ref, bx_ref,
                lam_ref, o_ref, hfin_ref, xs_ref, a_ref, b_ref, h_ref):
    zeros = jnp.zeros((CONV_PAD, LRU_BLOCK), F32)
    xs_ref[0:CONV_PAD, :] = zeros
    xs_ref[CONV_PAD + S:2 * CONV_PAD + S, :] = zeros
    xs_ref[CONV_PAD:CONV_PAD + S, :] = x_ref[...]

    neg_lam = -lam_ref[...]
    softplus = jnp.maximum(neg_lam, 0.0) + jnp.log1p(jnp.exp(-jnp.abs(neg_lam)))

    for c in range(S // tchunk):
        t0 = c * tchunk
        xc = cb_ref[...]
        for j in range(CONV_W):
            xc = xc + xs_ref[pl.ds(CONV_PAD + t0 + j - CONV_LEFT, tchunk), :] * cw_ref[j:j + 1, :]
        xb = xc.astype(BF)
        for d in range(2):
            r = jax.nn.sigmoid(jnp.dot(xb, wa_ref[d], preferred_element_type=F32) + ba_ref[d:d + 1, :])
            i = jax.nn.sigmoid(jnp.dot(xb, wx_ref[d], preferred_element_type=F32) + bx_ref[d:d + 1, :])
            log_a = -LRU_C * r * softplus[d:d + 1, :]
            a_ref[d, pl.ds(t0, tchunk), :] = jnp.exp(log_a)
            b_ref[d, pl.ds(t0, tchunk), :] = jnp.sqrt(_neg_expm1(2.0 * log_a)) * (i * xc)

    n_tiles = S // SUBLANES

    def body(i, carry):
        hf, hr = carry
        tf = pl.multiple_of(i * SUBLANES, SUBLANES)
        af, bf = _tile_scan(a_ref[0, pl.ds(tf, SUBLANES), :], b_ref[0, pl.ds(tf, SUBLANES), :], False)
        h = bf + af * hf
        h_ref[0, pl.ds(tf, SUBLANES), :] = h
        hf = h[SUBLANES - 1:SUBLANES, :]
        tr = pl.multiple_of((n_tiles - 1 - i) * SUBLANES, SUBLANES)
        ar, br = _tile_scan(a_ref[1, pl.ds(tr, SUBLANES), :], b_ref[1, pl.ds(tr, SUBLANES), :], True)
        h = br + ar * hr
        h_ref[1, pl.ds(tr, SUBLANES), :] = h
        hr = h[0:1, :]
        return hf, hr

    hf, hr = lax.fori_loop(0, n_tiles, body, (h0_ref[0:1, :], h0_ref[1:2, :]), unroll=unroll)
    hfin_ref[0:1, :] = hf
    hfin_ref[1:2, :] = hr
    o_ref[...] = ((h_ref[0] + h_ref[1]) * gy_ref[...]).astype(o_ref.dtype)


def lru_mix(xseg, gy, h0, conv_w, conv_b, wa, ba, wx, bx, lam, l, tchunk=512, unroll=4):
    B, S, W = xseg.shape
    tchunk = min(tchunk, S)
    nblk = W // LRU_BLOCK
    seq = pl.BlockSpec((None, S, LRU_BLOCK), lambda b, n: (b, 0, n))
    st = pl.BlockSpec((None, 2, LRU_BLOCK), lambda b, n: (b, 0, n))
    vec2 = pl.BlockSpec((None, 2, LRU_BLOCK), lambda b, n: (l, 0, n))
    wblk = pl.BlockSpec((None, 2, None, LRU_BLOCK, LRU_BLOCK), lambda b, n: (l, 0, n, 0, 0))
    return pl.pallas_call(
        functools.partial(_lru_kernel, S, tchunk, unroll),
        out_shape=(jax.ShapeDtypeStruct((B, S, W), BF), jax.ShapeDtypeStruct((B, 2, W), F32)),
        grid=(B, nblk),
        in_specs=[seq, seq, st,
                  pl.BlockSpec((None, CONV_W, LRU_BLOCK), lambda b, n: (l, 0, n)),
                  pl.BlockSpec((None, 1, LRU_BLOCK), lambda b, n: (l, 0, n)),
                  wblk, vec2, wblk, vec2, vec2],
        out_specs=(seq, st),
        scratch_shapes=[pltpu.VMEM((S + 2 * CONV_PAD, LRU_BLOCK), F32),
                        pltpu.VMEM((2, S, LRU_BLOCK), F32),
                        pltpu.VMEM((2, S, LRU_BLOCK), F32),
                        pltpu.VMEM((2, S, LRU_BLOCK), F32)],
        compiler_params=_cparams("arbitrary", "arbitrary"),
        name="lru_mix",
    )(xseg, gy, h0, conv_w, conv_b.reshape(conv_b.shape[0], 1, W), wa, ba, wx, bx, lam)


def _dft_tables(n):
    k = jnp.arange(n, dtype=jnp.int32)
    ang = ((k[:, None] * k[None, :]) % n).astype(F32) * (2.0 * math.pi / n)
    return jnp.cos(ang), jnp.sin(ang)


def _chan_dft_kernel(f_ref, cs_ref, zc_ref, zs_ref):
    w = zc_ref.shape[-1]
    z = jnp.dot(f_ref[...], cs_ref[...], preferred_element_type=F32)
    zc_ref[...] = z[:, :w].astype(zc_ref.dtype)
    zs_ref[...] = z[:, w:].astype(zs_ref.dtype)


def chan_dft(f, cs, tm, out_dtype):
    G, S, W = f.shape
    gw = W // FNET_GROUPS
    blk = pl.BlockSpec((None, tm, gw), lambda g, m, n: (g, m, n))
    return pl.pallas_call(
        _chan_dft_kernel,
        out_shape=(jax.ShapeDtypeStruct((G, S, W), out_dtype),) * 2,
        grid=(G, S // tm, FNET_GROUPS),
        in_specs=[blk, pl.BlockSpec((gw, 2 * gw), lambda g, m, n: (0, 0))],
        out_specs=(blk, blk),
        compiler_params=_cparams("arbitrary", "arbitrary", "arbitrary"),
        name="chan_dft",
    )(f, cs)


def _time_dft_kernel(scale, c_ref, s_ref, zc_ref, zs_ref, o_ref):
    re = jnp.dot(c_ref[...], zc_ref[...], preferred_element_type=F32)
    re = re - jnp.dot(s_ref[...], zs_ref[...], preferred_element_type=F32)
    o_ref[...] = (re * scale).astype(o_ref.dtype)


def time_dft_dense(zc, zs, scale, tm=512, tn=512):
    B, T, W = zc.shape
    tm = min(tm, T)
    ct, st = (t.astype(BF) for t in _dft_tables(T))
    a_spec = pl.BlockSpec((tm, T), lambda b, n, m: (m, 0))
    z_spec = pl.BlockSpec((None, T, tn), lambda b, n, m: (b, 0, n))
    return pl.pallas_call(
        functools.partial(_time_dft_kernel, scale),
        out_shape=jax.ShapeDtypeStruct((B, T, W), BF),
        grid=(B, W // tn, T // tm),
        in_specs=[a_spec, a_spec, z_spec, z_spec],
        out_specs=pl.BlockSpec((None, tm, tn), lambda b, n, m: (b, m, n)),
        compiler_params=_cparams("arbitrary", "arbitrary", "arbitrary"),
        name="time_dft",
    )(ct, st, zc, zs)


def _dft_stage1_kernel(zc_ref, zs_ref, rot_ref, twc_ref, tws_ref, oc_ref, os_ref):
    n1 = zc_ref.shape[0]
    tn = zc_ref.shape[-1]
    rot = rot_ref[...]
    for j in range(zc_ref.shape[1]):
        z = jnp.concatenate([zc_ref[:, j, :], zs_ref[:, j, :]], axis=0).astype(BF)
        res = jnp.dot(rot, z, preferred_element_type=F32)
        ac, as_ = res[:n1], res[n1:]
        twc = jnp.tile(twc_ref[j], (1, tn // LANES))
        tws = jnp.tile(tws_ref[j], (1, tn // LANES))
        oc_ref[j] = (ac * twc - as_ * tws).astype(oc_ref.dtype)
        os_ref[j] = (as_ * twc + ac * tws).astype(os_ref.dtype)


def dft_stage1(zc, zs, n1, n2, tn=512):
    B, T, W = zc.shape
    c, s = _dft_tables(n1)
    rot = jnp.concatenate([jnp.concatenate([c, -s], axis=1), jnp.concatenate([s, c], axis=1)], axis=0).astype(BF)
    cc = jnp.arange(n1, dtype=jnp.int32)
    bb = jnp.arange(n2, dtype=jnp.int32)
    ang = (bb[:, None] * cc[None, :]).astype(F32) * (2.0 * math.pi / T)
    twc = jnp.broadcast_to(jnp.cos(ang)[:, :, None], (n2, n1, LANES))
    tws = jnp.broadcast_to(jnp.sin(ang)[:, :, None], (n2, n1, LANES))
    z_spec = pl.BlockSpec((None, n1, SUBLANES, tn), lambda b, i, n: (b, 0, i, n))
    tw_spec = pl.BlockSpec((SUBLANES, n1, LANES), lambda b, i, n: (i, 0, 0))
    o_spec = pl.BlockSpec((None, SUBLANES, n1, tn), lambda b, i, n: (b, i, 0, n))
    return pl.pallas_call(
        _dft_stage1_kernel,
        out_shape=(jax.ShapeDtypeStruct((B, n2, n1, W), BF),) * 2,
        grid=(B, n2 // SUBLANES, W // tn),
        in_specs=[z_spec, z_spec, pl.BlockSpec((2 * n1, 2 * n1), lambda b, i, n: (0, 0)), tw_spec, tw_spec],
        out_specs=(o_spec, o_spec),
        compiler_params=_cparams("arbitrary", "arbitrary", "arbitrary"),
        name="dft_stage1",
    )(zc.reshape(B, n1, n2, W), zs.reshape(B, n1, n2, W), rot, twc, tws)


BF16_SUBLANES = 16


def _dft_stage2_kernel(scale, ac_ref, as_ref, cs_ref, o_ref, ac_f32, as_f32, o_f32):
    cs = cs_ref[...]
    ac_f32[...] = ac_ref[...].astype(F32)
    as_f32[...] = as_ref[...].astype(F32)
    for j in range(ac_ref.shape[1]):
        a = jnp.concatenate([ac_f32[:, j, :], as_f32[:, j, :]], axis=0).astype(BF)
        o_f32[:, j, :] = jnp.dot(cs, a, preferred_element_type=F32) * scale
    o_ref[...] = o_f32[...].astype(o_ref.dtype)


def dft_stage2(ac, as_, scale, tn=512):
    B, n2, n1, W = ac.shape
    c, s = _dft_tables(n2)
    cs = jnp.concatenate([c, -s], axis=1).astype(BF)
    cblk = BF16_SUBLANES
    a_spec = pl.BlockSpec((None, n2, cblk, tn), lambda b, i, n: (b, 0, i, n))
    out = pl.pallas_call(
        functools.partial(_dft_stage2_kernel, scale),
        out_shape=jax.ShapeDtypeStruct((B, n2, n1, W), BF),
        grid=(B, n1 // cblk, W // tn),
        in_specs=[a_spec, a_spec, pl.BlockSpec((n2, 2 * n2), lambda b, i, n: (0, 0))],
        out_specs=a_spec,
        scratch_shapes=[pltpu.VMEM((n2, cblk, tn), F32)] * 3,
        compiler_params=_cparams("arbitrary", "arbitrary", "arbitrary"),
        name="dft_stage2",
    )(ac, as_, cs)
    return out.reshape(B, n2 * n1, W)


def _two_stage_dft_ok(T):
    n = math.isqrt(T)
    return n * n == T and n % BF16_SUBLANES == 0


def fourier_branch(f, cs_chan, tm, B, T, scale):
    W = f.shape[-1]
    if _two_stage_dft_ok(T):
        n = math.isqrt(T)
        zc, zs = chan_dft(f, cs_chan, tm, F32)
        ac, as_ = dft_stage1(zc.reshape(B, T, W), zs.reshape(B, T, W), n, n)
        return dft_stage2(ac, as_, scale)
    zc, zs = chan_dft(f, cs_chan, tm, BF)
    return time_dft_dense(zc.reshape(B, T, W), zs.reshape(B, T, W), scale)


def _merge_kernel(b0_ref, b1_ref, b2_ref, w_ref, g0_ref, g1_ref, g2_ref, o_ref):
    acc = g0_ref[...] * _dot(b0_ref[...], w_ref[0])
    acc = acc + g1_ref[...] * _dot(b1_ref[...], w_ref[1])
    acc = acc + g2_ref[...] * _dot(b2_ref[...], w_ref[2])
    o_ref[...] = acc.astype(o_ref.dtype)


def gated_merge(branches, w_branch, l, gates, tm, tn=256):
    G, S, K = branches[0].shape
    D = w_branch.shape[-1]
    nb = D // tn
    b_spec = pl.BlockSpec((None, tm, K), lambda g, m, n: (g, m, 0))

    def g_spec(j):
        return pl.BlockSpec((None, tm, tn), lambda g, m, n: (g, m, j * nb + n))

    return pl.pallas_call(
        _merge_kernel,
        out_shape=jax.ShapeDtypeStruct((G, S, D), BF),
        grid=(G, S // tm, nb),
        in_specs=[b_spec, b_spec, b_spec,
                  pl.BlockSpec((None, N_BRANCH, K, tn), lambda g, m, n: (l, 0, 0, n)),
                  g_spec(0), g_spec(1), g_spec(2)],
        out_specs=pl.BlockSpec((None, tm, tn), lambda g, m, n: (g, m, n)),
        compiler_params=_cparams("arbitrary", "arbitrary", "arbitrary"),
        name="gated_merge",
    )(*branches, w_branch, gates, gates, gates)


def _rope_tables(n_tokens):
    rows = n_tokens // GRID_W
    row = jnp.repeat(jnp.arange(rows, dtype=F32), GRID_W)
    col = jnp.tile(jnp.arange(GRID_W, dtype=F32), rows)
    axis_dim = HEAD_DIM // 2
    inv_freq = ROPE_THETA ** (-jnp.arange(0, axis_dim, 2, dtype=F32) / axis_dim)
    ar, ac = row[:, None] * inv_freq, col[:, None] * inv_freq
    cos = jnp.concatenate([jnp.cos(ar), jnp.cos(ar), jnp.cos(ac), jnp.cos(ac)], axis=-1)
    sin = jnp.concatenate([-jnp.sin(ar), jnp.sin(ar), -jnp.sin(ac), jnp.sin(ac)], axis=-1)
    return cos, sin


def kernel(x, c, ctx, c_ctx, w_ada, b_ada, g_norm, w_ff_in, w_ff_out, w_in, b_gate, q_gain, k_gain,
           conv_w, conv_b, lru_wa, lru_ba, lru_wx, lru_bx, lru_lam, w_branch, w_out, g_final):
    B, S, D = x.shape
    Tc = ctx.shape[1]
    depth = w_ada.shape[0]
    lru_w = conv_w.shape[-1]
    fnet_w = D // 2
    attn_w = (D // 256) * HEAD_DIM
    kv_w = attn_w // Q_GROUP
    v0 = attn_w + kv_w
    x0 = v0 + kv_w
    y0 = x0 + lru_w
    f0 = y0 + lru_w
    g0 = f0 + fnet_w

    lru_wa_b = lru_wa.astype(BF)
    lru_wx_b = lru_wx.astype(BF)

    n_rows = SUBLANES
    c_rows = jnp.concatenate([c, c_ctx[None, :], jnp.zeros((n_rows - B - 1, D), F32)], axis=0)
    mod = ada_mod(c_rows, w_ada, b_ada).reshape(depth, n_rows, N_MOD, 1, D)
    mod_lat = mod[:, :B]
    mod_ctx = mod[:, B:B + 1]

    cos_l, sin_l = _rope_tables(S)
    cos_c, sin_c = jnp.ones((B * Tc, HEAD_DIM), F32), jnp.zeros((B * Tc, HEAD_DIM), F32)
    q_fold = HEAD_DIM ** -0.5 * math.log2(math.e)
    gain_qk = jnp.concatenate([jnp.tile(q_gain * q_fold, (1, attn_w // HEAD_DIM)),
                               jnp.tile(k_gain, (1, kv_w // HEAD_DIM))], axis=1)
    gw = fnet_w // FNET_GROUPS
    cg, sg = _dft_tables(gw)
    cs_chan = jnp.concatenate([cg, sg], axis=1).astype(BF)

    tm_l = min(1024, S)
    tm_c = B * Tc
    xl = x
    xc = ctx.reshape(1, B * Tc, D)
    streams = {"lat": (tm_l, mod_lat), "ctx": (tm_c, mod_ctx)}

    def ffn(xs, which, l, i, k_mod):
        tm, mods = streams[which]
        h = norm_mod(xs, g_norm[l, 2 * i], mods[l], k_mod, k_mod + 1)
        act = swiglu_in(h, w_ff_in, l, i, tm)
        return resid_out(act, w_ff_out, (l, i), xs, mods[l], k_mod + 2, 0.5, tm)

    for l in range(depth):
        last = l == depth - 1
        xl = ffn(xl, "lat", l, 0, 0)
        xc = ffn(xc, "ctx", l, 0, 0)

        hl = norm_mod(xl, g_norm[l, 1], mod_lat[l], 3, 4)
        hc = norm_mod(xc, g_norm[l, 1], mod_ctx[l], 3, 4)
        gain = gain_qk[l:l + 1]
        bias_g = b_gate[l].reshape(1, N_BRANCH * D)

        qk_c = proj_qk(hc, w_in, l, attn_w + kv_w, gain, cos_c, sin_c, B * Tc, tm_c).reshape(B, Tc, -1)
        v_c = proj(hc, w_in, l, v0, kv_w, BF, tm_c).reshape(B, Tc, kv_w)
        xs_c = proj(hc, w_in, l, x0, lru_w, F32, tm_c).reshape(B, Tc, lru_w)
        if last:
            gy_c = jnp.zeros((B, Tc, lru_w), F32)
        else:
            gy_c = proj(hc, w_in, l, y0, lru_w, F32, tm_c, act="gelu").reshape(B, Tc, lru_w)
        lru_args = (conv_w, conv_b, lru_wa_b, lru_ba, lru_wx_b, lru_bx, lru_lam, l)
        lru_c, h_fin = lru_mix(xs_c, gy_c, jnp.zeros((B, 2, lru_w), F32), *lru_args)

        qk_l = proj_qk(hl, w_in, l, attn_w + kv_w, gain, cos_l, sin_l, S, tm_l)
        v_l = proj(hl, w_in, l, v0, kv_w, BF, tm_l)
        xs_l = proj(hl, w_in, l, x0, lru_w, F32, tm_l)
        gy_l = proj(hl, w_in, l, y0, lru_w, F32, tm_l, act="gelu")
        f_l = proj(hl, w_in, l, f0, fnet_w, BF, tm_l)
        gate_l = proj(hl, w_in, l, g0, N_BRANCH * D, F32, tm_l, act="sigmoid_bias", bias=bias_g)

        attn_l = attention(qk_l, v_l, qk_c, v_c, attn_w, True)
        lru_l, _ = lru_mix(xs_l, gy_l, h_fin, *lru_args)
        four_l = fourier_branch(f_l, cs_chan, tm_l, B, S, 1.0 / math.sqrt(S * gw))
        merged = gated_merge((attn_l, lru_l, four_l), w_branch, l, gate_l, tm_l)
        xl = resid_out(merged, w_out, (l,), xl, mod_lat[l], 5, 1.0, tm_l)
        xl = ffn(xl, "lat", l, 1, 6)

        if not last:
            f_c = proj(hc, w_in, l, f0, fnet_w, BF, tm_c)
            gate_c = proj(hc, w_in, l, g0, N_BRANCH * D, F32, tm_c, act="sigmoid_bias", bias=bias_g)
            attn_c = attention(qk_c, v_c, qk_c, v_c, attn_w, False)
            four_c = fourier_branch(f_c, cs_chan, tm_c, B, Tc, 1.0 / math.sqrt(Tc * gw))
            merged_c = gated_merge((attn_c.reshape(1, B * Tc, attn_w), lru_c.reshape(1, B * Tc, lru_w),
                                    four_c.reshape(1, B * Tc, fnet_w)), w_branch, l, gate_c, tm_c)
            xc = resid_out(merged_c, w_out, (l,), xc, mod_ctx[l], 5, 1.0, tm_c)
            xc = ffn(xc, "ctx", l, 1, 6)

    return final_norm(xl, g_final)
```

```python
import functools
import math

import jax
import jax.numpy as jnp
from jax import lax
from jax.experimental import pallas as pl
from jax.experimental.pallas import tpu as pltpu

F32 = jnp.float32
BF = jnp.bfloat16

VMEM_LIMIT_BYTES = 56 * 1024 * 1024
LANES = 128
SUBLANES = 8

HEAD_DIM = 128
Q_GROUP = 4
GRID_W = 64
CONV_W = 4
CONV_LEFT = 2
LRU_C = 8.0
LRU_BLOCK = 128
FNET_GROUPS = 4
N_BRANCH = 3
N_MOD = 9
ROPE_THETA = 10000.0
EPS = 1e-6
CONV_PAD = 8


def _cparams(*sem):
    return pltpu.CompilerParams(dimension_semantics=sem, vmem_limit_bytes=VMEM_LIMIT_BYTES)


def _silu(a):
    return a * jax.nn.sigmoid(a)


MATMUL_VMEM_BUDGET = 55 * 1024 * 1024


def _plan_tiles(tm_max, K, w_tiles, io_bytes_per_elem, tmp_bytes_per_elem, extra_bytes_per_row=0):
    tm = tm_max
    while tm >= 256:
        for tn in (512, 256):
            need = (2 * tm * K * 2 + w_tiles * K * tn * (2 * 4 + 2)
                    + tm * tn * (2 * io_bytes_per_elem + tmp_bytes_per_elem) + tm * extra_bytes_per_row)
            if need <= MATMUL_VMEM_BUDGET:
                return tm, tn
        tm //= 2
    raise ValueError("no matmul tiling fits VMEM")


def _lhs_spec(tm, K):
    return pl.BlockSpec((None, tm, K), lambda g, m, n: (g, m, 0))


def _dot(a, w):
    return jnp.dot(a, w.astype(BF), preferred_element_type=F32)


def _ada_kernel(c_ref, w_ref, b_ref, o_ref):
    s = _silu(c_ref[...]).astype(BF)
    o_ref[...] = _dot(s, w_ref[...]) + b_ref[...]


def ada_mod(c_rows, w_ada, b_ada, tn=1024):
    depth, d, n = w_ada.shape
    rows = c_rows.shape[0]
    return pl.pallas_call(
        _ada_kernel,
        out_shape=jax.ShapeDtypeStruct((depth, rows, n), F32),
        grid=(depth, n // tn),
        in_specs=[pl.BlockSpec((rows, d), lambda l, j: (0, 0)),
                  pl.BlockSpec((None, d, tn), lambda l, j: (l, 0, j)),
                  pl.BlockSpec((None, 1, tn), lambda l, j: (l, 0, j))],
        out_specs=pl.BlockSpec((None, rows, tn), lambda l, j: (l, 0, j)),
        compiler_params=_cparams("arbitrary", "arbitrary"),
        name="ada_mod",
    )(c_rows, w_ada, b_ada.reshape(depth, 1, n))


def _norm_mod_kernel(x_ref, g_ref, sh_ref, sc_ref, o_ref):
    x = x_ref[...]
    ms = jnp.mean(x * x, axis=-1, keepdims=True)
    y = x_ref[...] * lax.rsqrt(ms + EPS) * g_ref[...]
    o_ref[...] = (y * (1.0 + sc_ref[...]) + sh_ref[...]).astype(o_ref.dtype)


def norm_mod(x, g, mod, k_shift, k_scale, ts=256):
    G, S, D = x.shape
    ts = min(ts, S)
    return pl.pallas_call(
        _norm_mod_kernel,
        out_shape=jax.ShapeDtypeStruct((G, S, D), BF),
        grid=(G, S // ts),
        in_specs=[pl.BlockSpec((None, ts, D), lambda g_, i: (g_, i, 0)),
                  pl.BlockSpec((1, D), lambda g_, i: (0, 0)),
                  pl.BlockSpec((None, None, 1, D), lambda g_, i: (g_, k_shift, 0, 0)),
                  pl.BlockSpec((None, None, 1, D), lambda g_, i: (g_, k_scale, 0, 0))],
        out_specs=pl.BlockSpec((None, ts, D), lambda g_, i: (g_, i, 0)),
        compiler_params=_cparams("arbitrary", "arbitrary"),
        name="norm_mod",
    )(x, g.reshape(1, D), mod, mod)


def _final_norm_kernel(x_ref, g_ref, o_ref):
    x = x_ref[...]
    ms = jnp.mean(x * x, axis=-1, keepdims=True)
    o_ref[...] = x * lax.rsqrt(ms + EPS) * g_ref[...]


def final_norm(x, g, ts=256):
    G, S, D = x.shape
    return pl.pallas_call(
        _final_norm_kernel,
        out_shape=jax.ShapeDtypeStruct((G, S, D), F32),
        grid=(G, S // ts),
        in_specs=[pl.BlockSpec((None, ts, D), lambda g_, i: (g_, i, 0)),
                  pl.BlockSpec((1, D), lambda g_, i: (0, 0))],
        out_specs=pl.BlockSpec((None, ts, D), lambda g_, i: (g_, i, 0)),
        compiler_params=_cparams("arbitrary", "arbitrary"),
        name="final_norm",
    )(x, g.reshape(1, D))


def _swiglu_kernel(h_ref, wa_ref, wb_ref, o_ref):
    h = h_ref[...]
    a = _dot(h, wa_ref[...])
    b = _dot(h, wb_ref[...])
    o_ref[...] = (_silu(a) * b).astype(o_ref.dtype)


def swiglu_in(h, w, l, i, tm):
    G, S, K = h.shape
    F = w.shape[-1] // 2
    tm, tn = _plan_tiles(tm, K, 2, 2, 0)
    nb = F // tn
    return pl.pallas_call(
        _swiglu_kernel,
        out_shape=jax.ShapeDtypeStruct((G, S, F), BF),
        grid=(G, S // tm, nb),
        in_specs=[_lhs_spec(tm, K),
                  pl.BlockSpec((None, None, K, tn), lambda g, m, n: (l, i, 0, n)),
                  pl.BlockSpec((None, None, K, tn), lambda g, m, n: (l, i, 0, n + nb))],
        out_specs=pl.BlockSpec((None, tm, tn), lambda g, m, n: (g, m, n)),
        compiler_params=_cparams("arbitrary", "arbitrary", "arbitrary"),
        name="swiglu_in",
    )(h, w, w)


def _resid_kernel(coef, a_ref, w_ref, x_ref, g_ref, o_ref):
    y = _dot(a_ref[...], w_ref[...])
    o_ref[...] = x_ref[...] + (coef * g_ref[...]) * y


def resid_out(a, w, w_idx, x, mod, k_gate, coef, tm):
    G, S, K = a.shape
    D = x.shape[-1]
    tm, tn = _plan_tiles(tm, K, 1, 8, 16)
    nlead = len(w_idx)
    w_block = (None,) * nlead + (K, tn)
    return pl.pallas_call(
        functools.partial(_resid_kernel, coef),
        out_shape=jax.ShapeDtypeStruct((G, S, D), F32),
        grid=(G, S // tm, D // tn),
        in_specs=[_lhs_spec(tm, K),
                  pl.BlockSpec(w_block, lambda g, m, n: tuple(w_idx) + (0, n)),
                  pl.BlockSpec((None, tm, tn), lambda g, m, n: (g, m, n)),
                  pl.BlockSpec((None, None, 1, tn), lambda g, m, n: (g, k_gate, 0, n))],
        out_specs=pl.BlockSpec((None, tm, tn), lambda g, m, n: (g, m, n)),
        compiler_params=_cparams("arbitrary", "arbitrary", "arbitrary"),
        name="resid_out",
    )(a, w, x, mod)


def _proj_kernel(act, h_ref, w_ref, *rest):
    o_ref = rest[-1]
    acc = _dot(h_ref[...], w_ref[...])
    if act == "gelu":
        acc = jax.nn.gelu(acc)
    elif act == "sigmoid_bias":
        acc = jax.nn.sigmoid(acc + rest[0][...])
    o_ref[...] = acc.astype(o_ref.dtype)


def proj(h, w_in, l, col0, ncols, out_dtype, tm, act="none", bias=None):
    G, S, K = h.shape
    tm, tn = _plan_tiles(tm, K, 1, jnp.dtype(out_dtype).itemsize, 8)
    cb = col0 // tn
    in_specs = [_lhs_spec(tm, K),
                pl.BlockSpec((None, K, tn), lambda g, m, n: (l, 0, cb + n))]
    args = [h, w_in]
    if bias is not None:
        in_specs.append(pl.BlockSpec((1, tn), lambda g, m, n: (0, n)))
        args.append(bias)
    return pl.pallas_call(
        functools.partial(_proj_kernel, act),
        out_shape=jax.ShapeDtypeStruct((G, S, ncols), out_dtype),
        grid=(G, S // tm, ncols // tn),
        in_specs=in_specs,
        out_specs=pl.BlockSpec((None, tm, tn), lambda g, m, n: (g, m, n)),
        compiler_params=_cparams("arbitrary", "arbitrary", "arbitrary"),
        name="proj_" + act,
    )(*args)


def _qk_kernel(h_ref, w_ref, gain_ref, cos_ref, sin_ref, swap_ref, o_ref):
    acc = _dot(h_ref[...], w_ref[...])
    cos = cos_ref[...]
    sin = sin_ref[...]
    swap = swap_ref[...]
    for hd in range(acc.shape[1] // HEAD_DIM):
        cols = slice(hd * HEAD_DIM, (hd + 1) * HEAD_DIM)
        v = acc[:, cols]
        ms = jnp.mean(v * v, axis=-1, keepdims=True)
        y = v * lax.rsqrt(ms + EPS) * gain_ref[:, cols]
        hi = y.astype(BF)
        lo = (y - hi.astype(F32)).astype(BF)
        partner = (jnp.dot(hi, swap, preferred_element_type=F32)
                   + jnp.dot(lo, swap, preferred_element_type=F32))
        o_ref[:, cols] = (y * cos + partner * sin).astype(o_ref.dtype)


def proj_qk(h, w_in, l, ncols, gain, cos, sin, rows_per_seq, tm):
    G, S, K = h.shape
    tm, tn = _plan_tiles(tm, K, 1, 2, 8, extra_bytes_per_row=2 * 2 * HEAD_DIM * 4)
    seq_blocks = rows_per_seq // tm
    lane = jnp.arange(HEAD_DIM)
    quarter = HEAD_DIM // 4
    partner_lane = jnp.where((lane % (2 * quarter)) < quarter, lane + quarter, lane - quarter)
    swap = (lane[:, None] == partner_lane[None, :]).astype(BF)
    return pl.pallas_call(
        _qk_kernel,
        out_shape=jax.ShapeDtypeStruct((G, S, ncols), BF),
        grid=(G, S // tm, ncols // tn),
        in_specs=[_lhs_spec(tm, K),
                  pl.BlockSpec((None, K, tn), lambda g, m, n: (l, 0, n)),
                  pl.BlockSpec((1, tn), lambda g, m, n: (0, n)),
                  pl.BlockSpec((tm, HEAD_DIM), lambda g, m, n: (m % seq_blocks, 0)),
                  pl.BlockSpec((tm, HEAD_DIM), lambda g, m, n: (m % seq_blocks, 0)),
                  pl.BlockSpec((HEAD_DIM, HEAD_DIM), lambda g, m, n: (0, 0))],
        out_specs=pl.BlockSpec((None, tm, tn), lambda g, m, n: (g, m, n)),
        compiler_params=_cparams("arbitrary", "arbitrary", "arbitrary"),
        name="proj_qk",
    )(h, w_in, gain, cos, sin, swap)


def _attn_kernel(q_ref, kc_ref, vc_ref, *rest):
    o_ref = rest[-1]
    if len(rest) == 3:
        k = jnp.concatenate([kc_ref[...], rest[0][...]], axis=0)
        v = jnp.concatenate([vc_ref[...], rest[1][...]], axis=0)
    else:
        k, v = kc_ref[...], vc_ref[...]
    v_ones = jnp.concatenate([v, jnp.ones_like(v)], axis=1)
    for g in range(Q_GROUP):
        cols = slice(g * HEAD_DIM, (g + 1) * HEAD_DIM)
        s = lax.dot_general(q_ref[:, cols], k, (((1,), (1,)), ((), ())), preferred_element_type=F32)
        p = jnp.exp2(s - jnp.max(s, axis=-1, keepdims=True))
        acc = jnp.dot(p.astype(BF), v_ones, preferred_element_type=F32)
        o_ref[:, cols] = (acc[:, :HEAD_DIM] / acc[:, HEAD_DIM:]).astype(o_ref.dtype)


def attention(qk, v, qk_ctx, v_ctx, attn_width, with_latent_keys, tq=512):
    B, S, _ = qk.shape
    Tc = qk_ctx.shape[1]
    n_kv = v.shape[-1] // HEAD_DIM
    gw = Q_GROUP * HEAD_DIM
    kcol = attn_width // HEAD_DIM
    tq = min(tq, S)
    in_specs = [pl.BlockSpec((None, tq, gw), lambda b, h, i: (b, i, h)),
                pl.BlockSpec((None, Tc, HEAD_DIM), lambda b, h, i: (b, 0, kcol + h)),
                pl.BlockSpec((None, Tc, HEAD_DIM), lambda b, h, i: (b, 0, h))]
    args = [qk, qk_ctx, v_ctx]
    if with_latent_keys:
        in_specs += [pl.BlockSpec((None, S, HEAD_DIM), lambda b, h, i: (b, 0, kcol + h)),
                     pl.BlockSpec((None, S, HEAD_DIM), lambda b, h, i: (b, 0, h))]
        args += [qk, v]
    return pl.pallas_call(
        _attn_kernel,
        out_shape=jax.ShapeDtypeStruct((B, S, attn_width), BF),
        grid=(B, n_kv, S // tq),
        in_specs=in_specs,
        out_specs=pl.BlockSpec((None, tq, gw), lambda b, h, i: (b, i, h)),
        compiler_params=_cparams("arbitrary", "arbitrary", "arbitrary"),
        name="attention",
    )(*args)


def _neg_expm1(y):
    poly = y * (1.0 / 720.0) + (1.0 / 120.0)
    for coef in (1.0 / 24.0, 1.0 / 6.0, 0.5, 1.0):
        poly = poly * y + coef
    return jnp.where(y > -0.0625, -(poly * y), 1.0 - jnp.exp(y))


def _tile_scan(a, b, reverse):
    row = lax.broadcasted_iota(jnp.int32, a.shape, 0)
    for d in (1, 2, 4):
        shift = SUBLANES - d if reverse else d
        valid = (row < SUBLANES - d) if reverse else (row >= d)
        a_prev = jnp.where(valid, pltpu.roll(a, shift, 0), 1.0)
        b_prev = jnp.where(valid, pltpu.roll(b, shift, 0), 0.0)
        b = a * b_prev + b
        a = a * a_prev
    return a, b


def _lru_kernel(S, tchunk, unroll, x_ref, gy_ref, h0_ref, cw_ref, cb_ref, wa_ref, ba_ref, wx_ref, bx_ref,
                lam_ref, o_ref, hfin_ref, xs_ref, a_ref, b_ref, h_ref):
    zeros = jnp.zeros((CONV_PAD, LRU_BLOCK), F32)
    xs_ref[0:CONV_PAD, :] = zeros
    xs_ref[CONV_PAD + S:2 * CONV_PAD + S, :] = zeros
    xs_ref[CONV_PAD:CONV_PAD + S, :] = x_ref[...]

    neg_lam = -lam_ref[...]
    softplus = jnp.maximum(neg_lam, 0.0) + jnp.log1p(jnp.exp(-jnp.abs(neg_lam)))

    for c in range(S // tchunk):
        t0 = c * tchunk
        xc = cb_ref[...]
        for j in range(CONV_W):
            xc = xc + xs_ref[pl.ds(CONV_PAD + t0 + j - CONV_LEFT, tchunk), :] * cw_ref[j:j + 1, :]
        xb = xc.astype(BF)
        for d in range(2):
            r = jax.nn.sigmoid(jnp.dot(xb, wa_ref[d], preferred_element_type=F32) + ba_ref[d:d + 1, :])
            i = jax.nn.sigmoid(jnp.dot(xb, wx_ref[d], preferred_element_type=F32) + bx_ref[d:d + 1, :])
            log_a = -LRU_C * r * softplus[d:d + 1, :]
            a_ref[d, pl.ds(t0, tchunk), :] = jnp.exp(log_a)
            b_ref[d, pl.ds(t0, tchunk), :] = jnp.sqrt(_neg_expm1(2.0 * log_a)) * (i * xc)

    n_tiles = S // SUBLANES

    def body(i, carry):
        hf, hr = carry
        tf = pl.multiple_of(i * SUBLANES, SUBLANES)
        af, bf = _tile_scan(a_ref[0, pl.ds(tf, SUBLANES), :], b_ref[0, pl.ds(tf, SUBLANES), :], False)
        h = bf + af * hf
        h_ref[0, pl.ds(tf, SUBLANES), :] = h
        hf = h[SUBLANES - 1:SUBLANES, :]
        tr = pl.multiple_of((n_tiles - 1 - i) * SUBLANES, SUBLANES)
        ar, br = _tile_scan(a_ref[1, pl.ds(tr, SUBLANES), :], b_ref[1, pl.ds(tr, SUBLANES), :], True)
        h = br + ar * hr
        h_ref[1, pl.ds(tr, SUBLANES), :] = h
        hr = h[0:1, :]
        return hf, hr

    hf, hr = lax.fori_loop(0, n_tiles, body, (h0_ref[0:1, :], h0_ref[1:2, :]), unroll=unroll)
    hfin_ref[0:1, :] = hf
    hfin_ref[1:2, :] = hr
    o_ref[...] = ((h_ref[0] + h_ref[1]) * gy_ref[...]).astype(o_ref.dtype)


def lru_mix(xseg, gy, h0, conv_w, conv_b, wa, ba, wx, bx, lam, l, tchunk=512, unroll=4):
    B, S, W = xseg.shape
    tchunk = min(tchunk, S)
    nblk = W // LRU_BLOCK
    seq = pl.BlockSpec((None, S, LRU_BLOCK), lambda b, n: (b, 0, n))
    st = pl.BlockSpec((None, 2, LRU_BLOCK), lambda b, n: (b, 0, n))
    vec2 = pl.BlockSpec((None, 2, LRU_BLOCK), lambda b, n: (l, 0, n))
    wblk = pl.BlockSpec((None, 2, None, LRU_BLOCK, LRU_BLOCK), lambda b, n: (l, 0, n, 0, 0))
    return pl.pallas_call(
        functools.partial(_lru_kernel, S, tchunk, unroll),
        out_shape=(jax.ShapeDtypeStruct((B, S, W), BF), jax.ShapeDtypeStruct((B, 2, W), F32)),
        grid=(B, nblk),
        in_specs=[seq, seq, st,
                  pl.BlockSpec((None, CONV_W, LRU_BLOCK), lambda b, n: (l, 0, n)),
                  pl.BlockSpec((None, 1, LRU_BLOCK), lambda b, n: (l, 0, n)),
                  wblk, vec2, wblk, vec2, vec2],
        out_specs=(seq, st),
        scratch_shapes=[pltpu.VMEM((S + 2 * CONV_PAD, LRU_BLOCK), F32),
                        pltpu.VMEM((2, S, LRU_BLOCK), F32),
                        pltpu.VMEM((2, S, LRU_BLOCK), F32),
                        pltpu.VMEM((2, S, LRU_BLOCK), F32)],
        compiler_params=_cparams("arbitrary", "arbitrary"),
        name="lru_mix",
    )(xseg, gy, h0, conv_w, conv_b.reshape(conv_b.shape[0], 1, W), wa, ba, wx, bx, lam)


def _dft_tables(n):
    k = jnp.arange(n, dtype=jnp.int32)
    ang = ((k[:, None] * k[None, :]) % n).astype(F32) * (2.0 * math.pi / n)
    return jnp.cos(ang), jnp.sin(ang)


def _chan_dft_kernel(f_ref, cs_ref, zc_ref, zs_ref):
    w = zc_ref.shape[-1]
    z = jnp.dot(f_ref[...], cs_ref[...], preferred_element_type=F32)
    zc_ref[...] = z[:, :w].astype(zc_ref.dtype)
    zs_ref[...] = z[:, w:].astype(zs_ref.dtype)


def chan_dft(f, cs, tm, out_dtype):
    G, S, W = f.shape
    gw = W // FNET_GROUPS
    blk = pl.BlockSpec((None, tm, gw), lambda g, m, n: (g, m, n))
    return pl.pallas_call(
        _chan_dft_kernel,
        out_shape=(jax.ShapeDtypeStruct((G, S, W), out_dtype),) * 2,
        grid=(G, S // tm, FNET_GROUPS),
        in_specs=[blk, pl.BlockSpec((gw, 2 * gw), lambda g, m, n: (0, 0))],
        out_specs=(blk, blk),
        compiler_params=_cparams("arbitrary", "arbitrary", "arbitrary"),
        name="chan_dft",
    )(f, cs)


def _time_dft_kernel(scale, c_ref, s_ref, zc_ref, zs_ref, o_ref):
    re = jnp.dot(c_ref[...], zc_ref[...], preferred_element_type=F32)
    re = re - jnp.dot(s_ref[...], zs_ref[...], preferred_element_type=F32)
    o_ref[...] = (re * scale).astype(o_ref.dtype)


def time_dft_dense(zc, zs, scale, tm=512, tn=512):
    B, T, W = zc.shape
    tm = min(tm, T)
    ct, st = (t.astype(BF) for t in _dft_tables(T))
    a_spec = pl.BlockSpec((tm, T), lambda b, n, m: (m, 0))
    z_spec = pl.BlockSpec((None, T, tn), lambda b, n, m: (b, 0, n))
    return pl.pallas_call(
        functools.partial(_time_dft_kernel, scale),
        out_shape=jax.ShapeDtypeStruct((B, T, W), BF),
        grid=(B, W // tn, T // tm),
        in_specs=[a_spec, a_spec, z_spec, z_spec],
        out_specs=pl.BlockSpec((None, tm, tn), lambda b, n, m: (b, m, n)),
        compiler_params=_cparams("arbitrary", "arbitrary", "arbitrary"),
        name="time_dft",
    )(ct, st, zc, zs)


def _dft_stage1_kernel(zc_ref, zs_ref, rot_ref, twc_ref, tws_ref, oc_ref, os_ref):
    n1 = zc_ref.shape[0]
    tn = zc_ref.shape[-1]
    rot = rot_ref[...]
    for j in range(zc_ref.shape[1]):
        z = jnp.concatenate([zc_ref[:, j, :], zs_ref[:, j, :]], axis=0).astype(BF)
        res = jnp.dot(rot, z, preferred_element_type=F32)
        ac, as_ = res[:n1], res[n1:]
        twc = jnp.tile(twc_ref[j], (1, tn // LANES))
        tws = jnp.tile(tws_ref[j], (1, tn // LANES))
        oc_ref[j] = (ac * twc - as_ * tws).astype(oc_ref.dtype)
        os_ref[j] = (as_ * twc + ac * tws).astype(os_ref.dtype)


def dft_stage1(zc, zs, n1, n2, tn=512):
    B, T, W = zc.shape
    c, s = _dft_tables(n1)
    rot = jnp.concatenate([jnp.concatenate([c, -s], axis=1), jnp.concatenate([s, c], axis=1)], axis=0).astype(BF)
    cc = jnp.arange(n1, dtype=jnp.int32)
    bb = jnp.arange(n2, dtype=jnp.int32)
    ang = (bb[:, None] * cc[None, :]).astype(F32) * (2.0 * math.pi / T)
    twc = jnp.broadcast_to(jnp.cos(ang)[:, :, None], (n2, n1, LANES))
    tws = jnp.broadcast_to(jnp.sin(ang)[:, :, None], (n2, n1, LANES))
    z_spec = pl.BlockSpec((None, n1, SUBLANES, tn), lambda b, i, n: (b, 0, i, n))
    tw_spec = pl.BlockSpec((SUBLANES, n1, LANES), lambda b, i, n: (i, 0, 0))
    o_spec = pl.BlockSpec((None, SUBLANES, n1, tn), lambda b, i, n: (b, i, 0, n))
    return pl.pallas_call(
        _dft_stage1_kernel,
        out_shape=(jax.ShapeDtypeStruct((B, n2, n1, W), BF),) * 2,
        grid=(B, n2 // SUBLANES, W // tn),
        in_specs=[z_spec, z_spec, pl.BlockSpec((2 * n1, 2 * n1), lambda b, i, n: (0, 0)), tw_spec, tw_spec],
        out_specs=(o_spec, o_spec),
        compiler_params=_cparams("arbitrary", "arbitrary", "arbitrary"),
        name="dft_stage1",
    )(zc.reshape(B, n1, n2, W), zs.reshape(B, n1, n2, W), rot, twc, tws)


BF16_SUBLANES = 16


def _dft_stage2_kernel(scale, ac_ref, as_ref, cs_ref, o_ref, ac_f32, as_f32, o_f32):
    cs = cs_ref[...]
    ac_f32[...] = ac_ref[...].astype(F32)
    as_f32[...] = as_ref[...].astype(F32)
    for j in range(ac_ref.shape[1]):
        a = jnp.concatenate([ac_f32[:, j, :], as_f32[:, j, :]], axis=0).astype(BF)
        o_f32[:, j, :] = jnp.dot(cs, a, preferred_element_type=F32) * scale
    o_ref[...] = o_f32[...].astype(o_ref.dtype)


def dft_stage2(ac, as_, scale, tn=512):
    B, n2, n1, W = ac.shape
    c, s = _dft_tables(n2)
    cs = jnp.concatenate([c, -s], axis=1).astype(BF)
    cblk = BF16_SUBLANES
    a_spec = pl.BlockSpec((None, n2, cblk, tn), lambda b, i, n: (b, 0, i, n))
    out = pl.pallas_call(
        functools.partial(_dft_stage2_kernel, scale),
        out_shape=jax.ShapeDtypeStruct((B, n2, n1, W), BF),
        grid=(B, n1 // cblk, W // tn),
        in_specs=[a_spec, a_spec, pl.BlockSpec((n2, 2 * n2), lambda b, i, n: (0, 0))],
        out_specs=a_spec,
        scratch_shapes=[pltpu.VMEM((n2, cblk, tn), F32)] * 3,
        compiler_params=_cparams("arbitrary", "arbitrary", "arbitrary"),
        name="dft_stage2",
    )(ac, as_, cs)
    return out.reshape(B, n2 * n1, W)


def _two_stage_dft_ok(T):
    n = math.isqrt(T)
    return n * n == T and n % BF16_SUBLANES == 0


def fourier_branch(f, cs_chan, tm, B, T, scale):
    W = f.shape[-1]
    if _two_stage_dft_ok(T):
        n = math.isqrt(T)
        zc, zs = chan_dft(f, cs_chan, tm, F32)
        ac, as_ = dft_stage1(zc.reshape(B, T, W), zs.reshape(B, T, W), n, n)
        return dft_stage2(ac, as_, scale)
    zc, zs = chan_dft(f, cs_chan, tm, BF)
    return time_dft_dense(zc.reshape(B, T, W), zs.reshape(B, T, W), scale)


def _merge_kernel(b0_ref, b1_ref, b2_ref, w_ref, g0_ref, g1_ref, g2_ref, o_ref):
    acc = g0_ref[...] * _dot(b0_ref[...], w_ref[0])
    acc = acc + g1_ref[...] * _dot(b1_ref[...], w_ref[1])
    acc = acc + g2_ref[...] * _dot(b2_ref[...], w_ref[2])
    o_ref[...] = acc.astype(o_ref.dtype)


def gated_merge(branches, w_branch, l, gates, tm, tn=256):
    G, S, K = branches[0].shape
    tm = min(tm, 1024)
    D = w_branch.shape[-1]
    nb = D // tn
    b_spec = pl.BlockSpec((None, tm, K), lambda g, m, n: (g, m, 0))

    def g_spec(j):
        return pl.BlockSpec((None, tm, tn), lambda g, m, n: (g, m, j * nb + n))

    return pl.pallas_call(
        _merge_kernel,
        out_shape=jax.ShapeDtypeStruct((G, S, D), BF),
        grid=(G, S // tm, nb),
        in_specs=[b_spec, b_spec, b_spec,
                  pl.BlockSpec((None, N_BRANCH, K, tn), lambda g, m, n: (l, 0, 0, n)),
                  g_spec(0), g_spec(1), g_spec(2)],
        out_specs=pl.BlockSpec((None, tm, tn), lambda g, m, n: (g, m, n)),
        compiler_params=_cparams("arbitrary", "arbitrary", "arbitrary"),
        name="gated_merge",
    )(*branches, w_branch, gates, gates, gates)


def _rope_tables(n_tokens):
    rows = n_tokens // GRID_W
    row = jnp.repeat(jnp.arange(rows, dtype=F32), GRID_W)
    col = jnp.tile(jnp.arange(GRID_W, dtype=F32), rows)
    axis_dim = HEAD_DIM // 2
    inv_freq = ROPE_THETA ** (-jnp.arange(0, axis_dim, 2, dtype=F32) / axis_dim)
    ar, ac = row[:, None] * inv_freq, col[:, None] * inv_freq
    cos = jnp.concatenate([jnp.cos(ar), jnp.cos(ar), jnp.cos(ac), jnp.cos(ac)], axis=-1)
    sin = jnp.concatenate([-jnp.sin(ar), jnp.sin(ar), -jnp.sin(ac), jnp.sin(ac)], axis=-1)
    return cos, sin


def kernel(x, c, ctx, c_ctx, w_ada, b_ada, g_norm, w_ff_in, w_ff_out, w_in, b_gate, q_gain, k_gain,
           conv_w, conv_b, lru_wa, lru_ba, lru_wx, lru_bx, lru_lam, w_branch, w_out, g_final):
    B, S, D = x.shape
    Tc = ctx.shape[1]
    depth = w_ada.shape[0]
    lru_w = conv_w.shape[-1]
    fnet_w = D // 2
    attn_w = (D // 256) * HEAD_DIM
    kv_w = attn_w // Q_GROUP
    v0 = attn_w + kv_w
    x0 = v0 + kv_w
    y0 = x0 + lru_w
    f0 = y0 + lru_w
    g0 = f0 + fnet_w

    lru_wa_b = lru_wa.astype(BF)
    lru_wx_b = lru_wx.astype(BF)

    n_rows = SUBLANES
    c_rows = jnp.concatenate([c, c_ctx[None, :], jnp.zeros((n_rows - B - 1, D), F32)], axis=0)
    mod = ada_mod(c_rows, w_ada, b_ada).reshape(depth, n_rows, N_MOD, 1, D)
    mod_lat = mod[:, :B]
    mod_ctx = mod[:, B:B + 1]

    cos_l, sin_l = _rope_tables(S)
    cos_c, sin_c = jnp.ones((B * Tc, HEAD_DIM), F32), jnp.zeros((B * Tc, HEAD_DIM), F32)
    q_fold = HEAD_DIM ** -0.5 * math.log2(math.e)
    gain_qk = jnp.concatenate([jnp.tile(q_gain * q_fold, (1, attn_w // HEAD_DIM)),
                               jnp.tile(k_gain, (1, kv_w // HEAD_DIM))], axis=1)
    gw = fnet_w // FNET_GROUPS
    cg, sg = _dft_tables(gw)
    cs_chan = jnp.concatenate([cg, sg], axis=1).astype(BF)

    tm_l = min(2048, S)
    tm_c = B * Tc
    xl = x
    xc = ctx.reshape(1, B * Tc, D)
    streams = {"lat": (tm_l, mod_lat), "ctx": (tm_c, mod_ctx)}

    def ffn(xs, which, l, i, k_mod):
        tm, mods = streams[which]
        h = norm_mod(xs, g_norm[l, 2 * i], mods[l], k_mod, k_mod + 1)
        act = swiglu_in(h, w_ff_in, l, i, tm)
        return resid_out(act, w_ff_out, (l, i), xs, mods[l], k_mod + 2, 0.5, tm)

    for l in range(depth):
        last = l == depth - 1
        xl = ffn(xl, "lat", l, 0, 0)
        xc = ffn(xc, "ctx", l, 0, 0)

        hl = norm_mod(xl, g_norm[l, 1], mod_lat[l], 3, 4)
        hc = norm_mod(xc, g_norm[l, 1], mod_ctx[l], 3, 4)
        gain = gain_qk[l:l + 1]
        bias_g = b_gate[l].reshape(1, N_BRANCH * D)

        qk_c = proj_qk(hc, w_in, l, attn_w + kv_w, gain, cos_c, sin_c, B * Tc, tm_c).reshape(B, Tc, -1)
        v_c = proj(hc, w_in, l, v0, kv_w, BF, tm_c).reshape(B, Tc, kv_w)
        xs_c = proj(hc, w_in, l, x0, lru_w, F32, tm_c).reshape(B, Tc, lru_w)
        if last:
            gy_c = jnp.zeros((B, Tc, lru_w), F32)
        else:
            gy_c = proj(hc, w_in, l, y0, lru_w, F32, tm_c, act="gelu").reshape(B, Tc, lru_w)
        lru_args = (conv_w, conv_b, lru_wa_b, lru_ba, lru_wx_b, lru_bx, lru_lam, l)
        lru_c, h_fin = lru_mix(xs_c, gy_c, jnp.zeros((B, 2, lru_w), F32), *lru_args)

        qk_l = proj_qk(hl, w_in, l, attn_w + kv_w, gain, cos_l, sin_l, S, tm_l)
        v_l = proj(hl, w_in, l, v0, kv_w, BF, tm_l)
        xs_l = proj(hl, w_in, l, x0, lru_w, F32, tm_l)
        gy_l = proj(hl, w_in, l, y0, lru_w, F32, tm_l, act="gelu")
        f_l = proj(hl, w_in, l, f0, fnet_w, BF, tm_l)
        gate_l = proj(hl, w_in, l, g0, N_BRANCH * D, F32, tm_l, act="sigmoid_bias", bias=bias_g)

        attn_l = attention(qk_l, v_l, qk_c, v_c, attn_w, True)
        lru_l, _ = lru_mix(xs_l, gy_l, h_fin, *lru_args)
        four_l = fourier_branch(f_l, cs_chan, tm_l, B, S, 1.0 / math.sqrt(S * gw))
        merged = gated_merge((attn_l, lru_l, four_l), w_branch, l, gate_l, tm_l)
        xl = resid_out(merged, w_out, (l,), xl, mod_lat[l], 5, 1.0, tm_l)
        xl = ffn(xl, "lat", l, 1, 6)

        if not last:
            f_c = proj(hc, w_in, l, f0, fnet_w, BF, tm_c)
            gate_c = proj(hc, w_in, l, g0, N_BRANCH * D, F32, tm_c, act="sigmoid_bias", bias=bias_g)
            attn_c = attention(qk_c, v_c, qk_c, v_c, attn_w, False)
            four_c = fourier_branch(f_c, cs_chan, tm_c, B, Tc, 1.0 / math.sqrt(Tc * gw))
            merged_c = gated_merge((attn_c.reshape(1, B * Tc, attn_w), lru_c.reshape(1, B * Tc, lru_w),
                                    four_c.reshape(1, B * Tc, fnet_w)), w_branch, l, gate_c, tm_c)
            xc = resid_out(merged_c, w_out, (l,), xc, mod_ctx[l], 5, 1.0, tm_c)
            xc = ffn(xc, "ctx", l, 1, 6)

    return final_norm(xl, g_final)
```

```python
import functools
import math

import jax
import jax.numpy as jnp
from jax import lax
from jax.experimental import pallas as pl
from jax.experimental.pallas import tpu as pltpu

F32 = jnp.float32
BF = jnp.bfloat16

VMEM_LIMIT_BYTES = 56 * 1024 * 1024
LANES = 128
SUBLANES = 8
BF16_SUBLANES = 16

HEAD_DIM = 128
Q_GROUP = 4
GRID_W = 64
CONV_W = 4
CONV_LEFT = 2
LRU_C = 8.0
LRU_BLOCK = 128
FNET_GROUPS = 4
N_BRANCH = 3
N_MOD = 9
ROPE_THETA = 10000.0
EPS = 1e-6
CONV_PAD = 8


def _cparams(*sem):
    return pltpu.CompilerParams(dimension_semantics=sem, vmem_limit_bytes=VMEM_LIMIT_BYTES)


def _silu(a):
    return a * jax.nn.sigmoid(a)


MATMUL_VMEM_BUDGET = 55 * 1024 * 1024


def _plan_tiles(tm_max, K, w_dtype, w_tiles, io_bytes_per_elem, tmp_bytes_per_elem, extra_bytes_per_row=0,
                tn_options=(512, 256)):
    w_bytes = jnp.dtype(w_dtype).itemsize
    w_copy_bytes = 0 if w_dtype == BF else 2
    tm = tm_max
    while tm >= 256:
        for tn in tn_options:
            need = (2 * tm * K * 2 + w_tiles * K * tn * (2 * w_bytes + w_copy_bytes)
                    + tm * tn * (2 * io_bytes_per_elem + tmp_bytes_per_elem) + tm * extra_bytes_per_row)
            if need <= MATMUL_VMEM_BUDGET:
                return tm, tn
        tm //= 2
    raise ValueError("no matmul tiling fits VMEM")


def _lhs_spec(tm, K):
    return pl.BlockSpec((None, tm, K), lambda g, m, n: (g, m, 0))


def _dot(a, w):
    return jnp.dot(a, w.astype(BF), preferred_element_type=F32)


def _ada_kernel(c_ref, w_ref, b_ref, o_ref):
    s = _silu(c_ref[...]).astype(BF)
    o_ref[...] = _dot(s, w_ref[...]) + b_ref[...]


def ada_mod(c_rows, w_ada, b_ada, tn=1024):
    depth, d, n = w_ada.shape
    rows = c_rows.shape[0]
    return pl.pallas_call(
        _ada_kernel,
        out_shape=jax.ShapeDtypeStruct((depth, rows, n), F32),
        grid=(depth, n // tn),
        in_specs=[pl.BlockSpec((rows, d), lambda l, j: (0, 0)),
                  pl.BlockSpec((None, d, tn), lambda l, j: (l, 0, j)),
                  pl.BlockSpec((None, 1, tn), lambda l, j: (l, 0, j))],
        out_specs=pl.BlockSpec((None, rows, tn), lambda l, j: (l, 0, j)),
        compiler_params=_cparams("arbitrary", "arbitrary"),
        name="ada_mod",
    )(c_rows, w_ada, b_ada.reshape(depth, 1, n))


def _norm_mod_kernel(x_ref, g_ref, sh_ref, sc_ref, o_ref):
    x = x_ref[...]
    ms = jnp.mean(x * x, axis=-1, keepdims=True)
    y = x_ref[...] * lax.rsqrt(ms + EPS) * g_ref[...]
    o_ref[...] = (y * (1.0 + sc_ref[...]) + sh_ref[...]).astype(o_ref.dtype)


def norm_mod(x, g, mod, k_shift, k_scale, ts=256):
    G, S, D = x.shape
    ts = min(ts, S)
    return pl.pallas_call(
        _norm_mod_kernel,
        out_shape=jax.ShapeDtypeStruct((G, S, D), BF),
        grid=(G, S // ts),
        in_specs=[pl.BlockSpec((None, ts, D), lambda g_, i: (g_, i, 0)),
                  pl.BlockSpec((1, D), lambda g_, i: (0, 0)),
                  pl.BlockSpec((None, None, 1, D), lambda g_, i: (g_, k_shift, 0, 0)),
                  pl.BlockSpec((None, None, 1, D), lambda g_, i: (g_, k_scale, 0, 0))],
        out_specs=pl.BlockSpec((None, ts, D), lambda g_, i: (g_, i, 0)),
        compiler_params=_cparams("arbitrary", "arbitrary"),
        name="norm_mod",
    )(x, g.reshape(1, D), mod, mod)


def _final_norm_kernel(x_ref, g_ref, o_ref):
    x = x_ref[...]
    ms = jnp.mean(x * x, axis=-1, keepdims=True)
    o_ref[...] = x * lax.rsqrt(ms + EPS) * g_ref[...]


def final_norm(x, g, ts=256):
    G, S, D = x.shape
    return pl.pallas_call(
        _final_norm_kernel,
        out_shape=jax.ShapeDtypeStruct((G, S, D), F32),
        grid=(G, S // ts),
        in_specs=[pl.BlockSpec((None, ts, D), lambda g_, i: (g_, i, 0)),
                  pl.BlockSpec((1, D), lambda g_, i: (0, 0))],
        out_specs=pl.BlockSpec((None, ts, D), lambda g_, i: (g_, i, 0)),
        compiler_params=_cparams("arbitrary", "arbitrary"),
        name="final_norm",
    )(x, g.reshape(1, D))


def _swiglu_kernel(h_ref, wa_ref, wb_ref, o_ref):
    h = h_ref[...]
    a = _dot(h, wa_ref[...])
    b = _dot(h, wb_ref[...])
    o_ref[...] = (_silu(a) * b).astype(o_ref.dtype)


def _w_spec(w_idx, block, col_block):
    lead = tuple(w_idx)
    return pl.BlockSpec((None,) * len(lead) + block,
                        lambda g, m, n: lead + (0,) * (len(block) - 1) + (col_block(n),))


def swiglu_in(h, w, w_idx, tm):
    G, S, K = h.shape
    F = w.shape[-1] // 2
    tm, tn = _plan_tiles(tm, K, w.dtype, 2, 2, 0, tn_options=(256,))
    nb = F // tn
    return pl.pallas_call(
        _swiglu_kernel,
        out_shape=jax.ShapeDtypeStruct((G, S, F), BF),
        grid=(G, S // tm, nb),
        in_specs=[_lhs_spec(tm, K),
                  _w_spec(w_idx, (K, tn), lambda n: n),
                  _w_spec(w_idx, (K, tn), lambda n: n + nb)],
        out_specs=pl.BlockSpec((None, tm, tn), lambda g, m, n: (g, m, n)),
        compiler_params=_cparams("arbitrary", "arbitrary", "arbitrary"),
        name="swiglu_in",
    )(h, w, w)


def _resid_kernel(coef, a_ref, w_ref, x_ref, g_ref, o_ref):
    y = _dot(a_ref[...], w_ref[...])
    o_ref[...] = x_ref[...] + (coef * g_ref[...]) * y


def resid_out(a, w, w_idx, x, mod, k_gate, coef, tm):
    G, S, K = a.shape
    D = x.shape[-1]
    tm, tn = _plan_tiles(tm, K, w.dtype, 1, 8, 16)
    return pl.pallas_call(
        functools.partial(_resid_kernel, coef),
        out_shape=jax.ShapeDtypeStruct((G, S, D), F32),
        grid=(G, S // tm, D // tn),
        in_specs=[_lhs_spec(tm, K),
                  _w_spec(w_idx, (K, tn), lambda n: n),
                  pl.BlockSpec((None, tm, tn), lambda g, m, n: (g, m, n)),
                  pl.BlockSpec((None, None, 1, tn), lambda g, m, n: (g, k_gate, 0, n))],
        out_specs=pl.BlockSpec((None, tm, tn), lambda g, m, n: (g, m, n)),
        compiler_params=_cparams("arbitrary", "arbitrary", "arbitrary"),
        name="resid_out",
    )(a, w, x, mod)


def _proj_kernel(act, h_ref, w_ref, *rest):
    o_ref = rest[-1]
    acc = _dot(h_ref[...], w_ref[...])
    if act == "gelu":
        acc = jax.nn.gelu(acc)
    elif act == "sigmoid_bias":
        acc = jax.nn.sigmoid(acc + rest[0][...])
    o_ref[...] = acc.astype(o_ref.dtype)


def proj(h, w_in, l, col0, ncols, out_dtype, tm, act="none", bias=None):
    G, S, K = h.shape
    tm, tn = _plan_tiles(tm, K, w_in.dtype, 1, jnp.dtype(out_dtype).itemsize, 8)
    cb = col0 // tn
    in_specs = [_lhs_spec(tm, K),
                pl.BlockSpec((None, K, tn), lambda g, m, n: (l, 0, cb + n))]
    args = [h, w_in]
    if bias is not None:
        in_specs.append(pl.BlockSpec((1, tn), lambda g, m, n: (0, n)))
        args.append(bias)
    return pl.pallas_call(
        functools.partial(_proj_kernel, act),
        out_shape=jax.ShapeDtypeStruct((G, S, ncols), out_dtype),
        grid=(G, S // tm, ncols // tn),
        in_specs=in_specs,
        out_specs=pl.BlockSpec((None, tm, tn), lambda g, m, n: (g, m, n)),
        compiler_params=_cparams("arbitrary", "arbitrary", "arbitrary"),
        name="proj_" + act,
    )(*args)


def _qk_kernel(h_ref, w_ref, gain_ref, cos_ref, sin_ref, swap_ref, o_ref):
    acc = _dot(h_ref[...], w_ref[...])
    cos = cos_ref[...]
    sin = sin_ref[...]
    swap = swap_ref[...]
    for hd in range(acc.shape[1] // HEAD_DIM):
        cols = slice(hd * HEAD_DIM, (hd + 1) * HEAD_DIM)
        v = acc[:, cols]
        ms = jnp.mean(v * v, axis=-1, keepdims=True)
        y = v * lax.rsqrt(ms + EPS) * gain_ref[:, cols]
        hi = y.astype(BF)
        lo = (y - hi.astype(F32)).astype(BF)
        partner = (jnp.dot(hi, swap, preferred_element_type=F32)
                   + jnp.dot(lo, swap, preferred_element_type=F32))
        o_ref[:, cols] = (y * cos + partner * sin).astype(o_ref.dtype)


def proj_qk(h, w_in, l, ncols, gain, cos, sin, rows_per_seq, tm):
    G, S, K = h.shape
    tm, tn = _plan_tiles(tm, K, w_in.dtype, 1, 2, 8, extra_bytes_per_row=2 * 2 * HEAD_DIM * 4)
    seq_blocks = rows_per_seq // tm
    lane = jnp.arange(HEAD_DIM)
    quarter = HEAD_DIM // 4
    partner_lane = jnp.where((lane % (2 * quarter)) < quarter, lane + quarter, lane - quarter)
    swap = (lane[:, None] == partner_lane[None, :]).astype(BF)
    return pl.pallas_call(
        _qk_kernel,
        out_shape=jax.ShapeDtypeStruct((G, S, ncols), BF),
        grid=(G, S // tm, ncols // tn),
        in_specs=[_lhs_spec(tm, K),
                  pl.BlockSpec((None, K, tn), lambda g, m, n: (l, 0, n)),
                  pl.BlockSpec((1, tn), lambda g, m, n: (0, n)),
                  pl.BlockSpec((tm, HEAD_DIM), lambda g, m, n: (m % seq_blocks, 0)),
                  pl.BlockSpec((tm, HEAD_DIM), lambda g, m, n: (m % seq_blocks, 0)),
                  pl.BlockSpec((HEAD_DIM, HEAD_DIM), lambda g, m, n: (0, 0))],
        out_specs=pl.BlockSpec((None, tm, tn), lambda g, m, n: (g, m, n)),
        compiler_params=_cparams("arbitrary", "arbitrary", "arbitrary"),
        name="proj_qk",
    )(h, w_in, gain, cos, sin, swap)


def _attn_kernel(with_latent_keys, n_cast, q_ref, kc_ref, vc_ref, *rest):
    n_in = 2 if with_latent_keys else 0
    cast_in = rest[n_in:n_in + n_cast]
    o_ref = rest[n_in + n_cast]
    cast_out = rest[n_in + n_cast + 1:]
    for w_src, w_dst in zip(cast_in, cast_out):
        w_dst[...] = w_src[...].astype(w_dst.dtype)
    if with_latent_keys:
        k = jnp.concatenate([kc_ref[...], rest[0][...]], axis=0)
        v = jnp.concatenate([vc_ref[...], rest[1][...]], axis=0)
    else:
        k, v = kc_ref[...], vc_ref[...]
    v_ones = jnp.concatenate([v, jnp.ones_like(v)], axis=1)
    for g in range(Q_GROUP):
        cols = slice(g * HEAD_DIM, (g + 1) * HEAD_DIM)
        s = lax.dot_general(q_ref[:, cols], k, (((1,), (1,)), ((), ())), preferred_element_type=F32)
        p = jnp.exp2(s - jnp.max(s, axis=-1, keepdims=True))
        acc = jnp.dot(p.astype(BF), v_ones, preferred_element_type=F32)
        o_ref[:, cols] = (acc[:, :HEAD_DIM] / acc[:, HEAD_DIM:]).astype(o_ref.dtype)


def attention(qk, v, qk_ctx, v_ctx, attn_width, with_latent_keys, cast=(), tq=512):
    B, S, _ = qk.shape
    Tc = qk_ctx.shape[1]
    n_kv = v.shape[-1] // HEAD_DIM
    gw = Q_GROUP * HEAD_DIM
    kcol = attn_width // HEAD_DIM
    tq = min(tq, S)
    n_q = S // tq
    n_steps = B * n_kv * n_q
    in_specs = [pl.BlockSpec((None, tq, gw), lambda b, h, i: (b, i, h)),
                pl.BlockSpec((None, Tc, HEAD_DIM), lambda b, h, i: (b, 0, kcol + h)),
                pl.BlockSpec((None, Tc, HEAD_DIM), lambda b, h, i: (b, 0, h))]
    args = [qk, qk_ctx, v_ctx]
    if with_latent_keys:
        in_specs += [pl.BlockSpec((None, S, HEAD_DIM), lambda b, h, i: (b, 0, kcol + h)),
                     pl.BlockSpec((None, S, HEAD_DIM), lambda b, h, i: (b, 0, h))]
        args += [qk, v]
    out_shape = [jax.ShapeDtypeStruct((B, S, attn_width), BF)]
    out_specs = [pl.BlockSpec((None, tq, gw), lambda b, h, i: (b, i, h))]

    def step(b, h, i):
        return (b * n_kv + h) * n_q + i

    for w, w_idx in cast:
        rows, cols = w.shape[-2:]
        slab = rows // n_steps
        assert slab * n_steps == rows and slab % BF16_SUBLANES == 0, (rows, n_steps)
        lead = tuple(w_idx)
        in_specs.append(pl.BlockSpec((None,) * len(lead) + (slab, cols),
                                     lambda b, h, i, lead=lead: lead + (step(b, h, i), 0)))
        args.append(w)
        out_shape.append(jax.ShapeDtypeStruct((rows, cols), BF))
        out_specs.append(pl.BlockSpec((slab, cols), lambda b, h, i: (step(b, h, i), 0)))
    outs = pl.pallas_call(
        functools.partial(_attn_kernel, with_latent_keys, len(cast)),
        out_shape=tuple(out_shape),
        grid=(B, n_kv, n_q),
        in_specs=in_specs,
        out_specs=tuple(out_specs),
        compiler_params=_cparams("arbitrary", "arbitrary", "arbitrary"),
        name="attention",
    )(*args)
    return outs[0], list(outs[1:])


def _neg_expm1(y):
    poly = y * (1.0 / 720.0) + (1.0 / 120.0)
    for coef in (1.0 / 24.0, 1.0 / 6.0, 0.5, 1.0):
        poly = poly * y + coef
    return jnp.where(y > -0.0625, -(poly * y), 1.0 - jnp.exp(y))


def _tile_scan(a, b, reverse):
    row = lax.broadcasted_iota(jnp.int32, a.shape, 0)
    for d in (1, 2, 4):
        shift = SUBLANES - d if reverse else d
        valid = (row < SUBLANES - d) if reverse else (row >= d)
        a_prev = jnp.where(valid, pltpu.roll(a, shift, 0), 1.0)
        b_prev = jnp.where(valid, pltpu.roll(b, shift, 0), 0.0)
        b = a * b_prev + b
        a = a * a_prev
    return a, b


def _lru_kernel(S, tchunk, unroll, x_ref, gy_ref, h0_ref, cw_ref, cb_ref, wa_ref, ba_ref, wx_ref, bx_ref,
                lam_ref, o_ref, hfin_ref, xs_ref, a_ref, b_ref, h_ref):
    zeros = jnp.zeros((CONV_PAD, LRU_BLOCK), F32)
    xs_ref[0:CONV_PAD, :] = zeros
    xs_ref[CONV_PAD + S:2 * CONV_PAD + S, :] = zeros
    xs_ref[CONV_PAD:CONV_PAD + S, :] = x_ref[...]

    neg_lam = -lam_ref[...]
    softplus = jnp.maximum(neg_lam, 0.0) + jnp.log1p(jnp.exp(-jnp.abs(neg_lam)))

    for c in range(S // tchunk):
        t0 = c * tchunk
        xc = cb_ref[...]
        for j in range(CONV_W):
            xc = xc + xs_ref[pl.ds(CONV_PAD + t0 + j - CONV_LEFT, tchunk), :] * cw_ref[j:j + 1, :]
        xb = xc.astype(BF)
        for d in range(2):
            r = jax.nn.sigmoid(jnp.dot(xb, wa_ref[d], preferred_element_type=F32) + ba_ref[d:d + 1, :])
            i = jax.nn.sigmoid(jnp.dot(xb, wx_ref[d], preferred_element_type=F32) + bx_ref[d:d + 1, :])
            log_a = -LRU_C * r * softplus[d:d + 1, :]
            a_ref[d, pl.ds(t0, tchunk), :] = jnp.exp(log_a)
            b_ref[d, pl.ds(t0, tchunk), :] = jnp.sqrt(_neg_expm1(2.0 * log_a)) * (i * xc)

    n_tiles = S // SUBLANES

    def body(i, carry):
        hf, hr = carry
        tf = pl.multiple_of(i * SUBLANES, SUBLANES)
        af, bf = _tile_scan(a_ref[0, pl.ds(tf, SUBLANES), :], b_ref[0, pl.ds(tf, SUBLANES), :], False)
        h = bf + af * hf
        h_ref[0, pl.ds(tf, SUBLANES), :] = h
        hf = h[SUBLANES - 1:SUBLANES, :]
        tr = pl.multiple_of((n_tiles - 1 - i) * SUBLANES, SUBLANES)
        ar, br = _tile_scan(a_ref[1, pl.ds(tr, SUBLANES), :], b_ref[1, pl.ds(tr, SUBLANES), :], True)
        h = br + ar * hr
        h_ref[1, pl.ds(tr, SUBLANES), :] = h
        hr = h[0:1, :]
        return hf, hr

    hf, hr = lax.fori_loop(0, n_tiles, body, (h0_ref[0:1, :], h0_ref[1:2, :]), unroll=unroll)
    hfin_ref[0:1, :] = hf
    hfin_ref[1:2, :] = hr
    o_ref[...] = ((h_ref[0] + h_ref[1]) * gy_ref[...]).astype(o_ref.dtype)


def lru_mix(xseg, gy, h0, conv_w, conv_b, wa, ba, wx, bx, lam, l, tchunk=512, unroll=4):
    B, S, W = xseg.shape
    tchunk = min(tchunk, S)
    nblk = W // LRU_BLOCK
    seq = pl.BlockSpec((None, S, LRU_BLOCK), lambda b, n: (b, 0, n))
    st = pl.BlockSpec((None, 2, LRU_BLOCK), lambda b, n: (b, 0, n))
    vec2 = pl.BlockSpec((None, 2, LRU_BLOCK), lambda b, n: (l, 0, n))
    wblk = pl.BlockSpec((None, 2, None, LRU_BLOCK, LRU_BLOCK), lambda b, n: (l, 0, n, 0, 0))
    return pl.pallas_call(
        functools.partial(_lru_kernel, S, tchunk, unroll),
        out_shape=(jax.ShapeDtypeStruct((B, S, W), BF), jax.ShapeDtypeStruct((B, 2, W), F32)),
        grid=(B, nblk),
        in_specs=[seq, seq, st,
                  pl.BlockSpec((None, CONV_W, LRU_BLOCK), lambda b, n: (l, 0, n)),
                  pl.BlockSpec((None, 1, LRU_BLOCK), lambda b, n: (l, 0, n)),
                  wblk, vec2, wblk, vec2, vec2],
        out_specs=(seq, st),
        scratch_shapes=[pltpu.VMEM((S + 2 * CONV_PAD, LRU_BLOCK), F32),
                        pltpu.VMEM((2, S, LRU_BLOCK), F32),
                        pltpu.VMEM((2, S, LRU_BLOCK), F32),
                        pltpu.VMEM((2, S, LRU_BLOCK), F32)],
        compiler_params=_cparams("arbitrary", "arbitrary"),
        name="lru_mix",
    )(xseg, gy, h0, conv_w, conv_b.reshape(conv_b.shape[0], 1, W), wa, ba, wx, bx, lam)


def _dft_tables(n):
    k = jnp.arange(n, dtype=jnp.int32)
    ang = ((k[:, None] * k[None, :]) % n).astype(F32) * (2.0 * math.pi / n)
    return jnp.cos(ang), jnp.sin(ang)


def _chan_dft_kernel(f_ref, cs_ref, zc_ref, zs_ref):
    w = zc_ref.shape[-1]
    z = jnp.dot(f_ref[...], cs_ref[...], preferred_element_type=F32)
    zc_ref[...] = z[:, :w].astype(zc_ref.dtype)
    zs_ref[...] = z[:, w:].astype(zs_ref.dtype)


def chan_dft(f, cs, tm, out_dtype):
    G, S, W = f.shape
    gw = W // FNET_GROUPS
    blk = pl.BlockSpec((None, tm, gw), lambda g, m, n: (g, m, n))
    return pl.pallas_call(
        _chan_dft_kernel,
        out_shape=(jax.ShapeDtypeStruct((G, S, W), out_dtype),) * 2,
        grid=(G, S // tm, FNET_GROUPS),
        in_specs=[blk, pl.BlockSpec((gw, 2 * gw), lambda g, m, n: (0, 0))],
        out_specs=(blk, blk),
        compiler_params=_cparams("arbitrary", "arbitrary", "arbitrary"),
        name="chan_dft",
    )(f, cs)


def _time_dft_kernel(scale, c_ref, s_ref, zc_ref, zs_ref, o_ref):
    re = jnp.dot(c_ref[...], zc_ref[...], preferred_element_type=F32)
    re = re - jnp.dot(s_ref[...], zs_ref[...], preferred_element_type=F32)
    o_ref[...] = (re * scale).astype(o_ref.dtype)


def time_dft_dense(zc, zs, scale, tm=512, tn=512):
    B, T, W = zc.shape
    tm = min(tm, T)
    ct, st = (t.astype(BF) for t in _dft_tables(T))
    a_spec = pl.BlockSpec((tm, T), lambda b, n, m: (m, 0))
    z_spec = pl.BlockSpec((None, T, tn), lambda b, n, m: (b, 0, n))
    return pl.pallas_call(
        functools.partial(_time_dft_kernel, scale),
        out_shape=jax.ShapeDtypeStruct((B, T, W), BF),
        grid=(B, W // tn, T // tm),
        in_specs=[a_spec, a_spec, z_spec, z_spec],
        out_specs=pl.BlockSpec((None, tm, tn), lambda b, n, m: (b, m, n)),
        compiler_params=_cparams("arbitrary", "arbitrary", "arbitrary"),
        name="time_dft",
    )(ct, st, zc, zs)


def _dft_stage1_kernel(zc_ref, zs_ref, rot_ref, twc_ref, tws_ref, oc_ref, os_ref):
    n1 = zc_ref.shape[0]
    tn = zc_ref.shape[-1]
    rot = rot_ref[...]
    for j in range(zc_ref.shape[1]):
        z = jnp.concatenate([zc_ref[:, j, :], zs_ref[:, j, :]], axis=0).astype(BF)
        res = jnp.dot(rot, z, preferred_element_type=F32)
        ac, as_ = res[:n1], res[n1:]
        twc = jnp.tile(twc_ref[j], (1, tn // LANES))
        tws = jnp.tile(tws_ref[j], (1, tn // LANES))
        oc_ref[j] = (ac * twc - as_ * tws).astype(oc_ref.dtype)
        os_ref[j] = (as_ * twc + ac * tws).astype(os_ref.dtype)


def dft_stage1(zc, zs, n1, n2, tn=512):
    B, T, W = zc.shape
    c, s = _dft_tables(n1)
    rot = jnp.concatenate([jnp.concatenate([c, -s], axis=1), jnp.concatenate([s, c], axis=1)], axis=0).astype(BF)
    cc = jnp.arange(n1, dtype=jnp.int32)
    bb = jnp.arange(n2, dtype=jnp.int32)
    ang = (bb[:, None] * cc[None, :]).astype(F32) * (2.0 * math.pi / T)
    twc = jnp.broadcast_to(jnp.cos(ang)[:, :, None], (n2, n1, LANES))
    tws = jnp.broadcast_to(jnp.sin(ang)[:, :, None], (n2, n1, LANES))
    z_spec = pl.BlockSpec((None, n1, SUBLANES, tn), lambda b, i, n: (b, 0, i, n))
    tw_spec = pl.BlockSpec((SUBLANES, n1, LANES), lambda b, i, n: (i, 0, 0))
    o_spec = pl.BlockSpec((None, SUBLANES, n1, tn), lambda b, i, n: (b, i, 0, n))
    return pl.pallas_call(
        _dft_stage1_kernel,
        out_shape=(jax.ShapeDtypeStruct((B, n2, n1, W), BF),) * 2,
        grid=(B, n2 // SUBLANES, W // tn),
        in_specs=[z_spec, z_spec, pl.BlockSpec((2 * n1, 2 * n1), lambda b, i, n: (0, 0)), tw_spec, tw_spec],
        out_specs=(o_spec, o_spec),
        compiler_params=_cparams("arbitrary", "arbitrary", "arbitrary"),
        name="dft_stage1",
    )(zc.reshape(B, n1, n2, W), zs.reshape(B, n1, n2, W), rot, twc, tws)


def _dft_stage2_kernel(scale, ac_ref, as_ref, cs_ref, o_ref, ac_f32, as_f32, o_f32):
    cs = cs_ref[...]
    ac_f32[...] = ac_ref[...].astype(F32)
    as_f32[...] = as_ref[...].astype(F32)
    for j in range(ac_ref.shape[1]):
        a = jnp.concatenate([ac_f32[:, j, :], as_f32[:, j, :]], axis=0).astype(BF)
        o_f32[:, j, :] = jnp.dot(cs, a, preferred_element_type=F32) * scale
    o_ref[...] = o_f32[...].astype(o_ref.dtype)


def dft_stage2(ac, as_, scale, tn=512):
    B, n2, n1, W = ac.shape
    c, s = _dft_tables(n2)
    cs = jnp.concatenate([c, -s], axis=1).astype(BF)
    cblk = BF16_SUBLANES
    a_spec = pl.BlockSpec((None, n2, cblk, tn), lambda b, i, n: (b, 0, i, n))
    out = pl.pallas_call(
        functools.partial(_dft_stage2_kernel, scale),
        out_shape=jax.ShapeDtypeStruct((B, n2, n1, W), BF),
        grid=(B, n1 // cblk, W // tn),
        in_specs=[a_spec, a_spec, pl.BlockSpec((n2, 2 * n2), lambda b, i, n: (0, 0))],
        out_specs=a_spec,
        scratch_shapes=[pltpu.VMEM((n2, cblk, tn), F32)] * 3,
        compiler_params=_cparams("arbitrary", "arbitrary", "arbitrary"),
        name="dft_stage2",
    )(ac, as_, cs)
    return out.reshape(B, n2 * n1, W)


def _two_stage_dft_ok(T):
    n = math.isqrt(T)
    return n * n == T and n % BF16_SUBLANES == 0


def fourier_branch(f, cs_chan, tm, B, T, scale):
    W = f.shape[-1]
    if _two_stage_dft_ok(T):
        n = math.isqrt(T)
        zc, zs = chan_dft(f, cs_chan, tm, F32)
        ac, as_ = dft_stage1(zc.reshape(B, T, W), zs.reshape(B, T, W), n, n)
        return dft_stage2(ac, as_, scale)
    zc, zs = chan_dft(f, cs_chan, tm, BF)
    return time_dft_dense(zc.reshape(B, T, W), zs.reshape(B, T, W), scale)


def _merge_kernel(b0_ref, b1_ref, b2_ref, w_ref, g0_ref, g1_ref, g2_ref, o_ref):
    acc = g0_ref[...] * _dot(b0_ref[...], w_ref[0])
    acc = acc + g1_ref[...] * _dot(b1_ref[...], w_ref[1])
    acc = acc + g2_ref[...] * _dot(b2_ref[...], w_ref[2])
    o_ref[...] = acc.astype(o_ref.dtype)


def gated_merge(branches, w_branch, w_idx, gates, tm):
    G, S, K = branches[0].shape
    tm = min(tm, 1024)
    tn = 256
    D = w_branch.shape[-1]
    nb = D // tn
    b_spec = pl.BlockSpec((None, tm, K), lambda g, m, n: (g, m, 0))

    def g_spec(j):
        return pl.BlockSpec((None, tm, tn), lambda g, m, n: (g, m, j * nb + n))

    return pl.pallas_call(
        _merge_kernel,
        out_shape=jax.ShapeDtypeStruct((G, S, D), BF),
        grid=(G, S // tm, nb),
        in_specs=[b_spec, b_spec, b_spec,
                  _w_spec(w_idx, (N_BRANCH, K, tn), lambda n: n),
                  g_spec(0), g_spec(1), g_spec(2)],
        out_specs=pl.BlockSpec((None, tm, tn), lambda g, m, n: (g, m, n)),
        compiler_params=_cparams("arbitrary", "arbitrary", "arbitrary"),
        name="gated_merge",
    )(*branches, w_branch, gates, gates, gates)


def _rope_tables(n_tokens):
    rows = n_tokens // GRID_W
    row = jnp.repeat(jnp.arange(rows, dtype=F32), GRID_W)
    col = jnp.tile(jnp.arange(GRID_W, dtype=F32), rows)
    axis_dim = HEAD_DIM // 2
    inv_freq = ROPE_THETA ** (-jnp.arange(0, axis_dim, 2, dtype=F32) / axis_dim)
    ar, ac = row[:, None] * inv_freq, col[:, None] * inv_freq
    cos = jnp.concatenate([jnp.cos(ar), jnp.cos(ar), jnp.cos(ac), jnp.cos(ac)], axis=-1)
    sin = jnp.concatenate([-jnp.sin(ar), jnp.sin(ar), -jnp.sin(ac), jnp.sin(ac)], axis=-1)
    return cos, sin


def kernel(x, c, ctx, c_ctx, w_ada, b_ada, g_norm, w_ff_in, w_ff_out, w_in, b_gate, q_gain, k_gain,
           conv_w, conv_b, lru_wa, lru_ba, lru_wx, lru_bx, lru_lam, w_branch, w_out, g_final):
    B, S, D = x.shape
    Tc = ctx.shape[1]
    depth = w_ada.shape[0]
    lru_w = conv_w.shape[-1]
    fnet_w = D // 2
    attn_w = (D // 256) * HEAD_DIM
    kv_w = attn_w // Q_GROUP
    v0 = attn_w + kv_w
    x0 = v0 + kv_w
    y0 = x0 + lru_w
    f0 = y0 + lru_w
    g0 = f0 + fnet_w

    lru_wa_b = lru_wa.astype(BF)
    lru_wx_b = lru_wx.astype(BF)

    n_rows = SUBLANES
    c_rows = jnp.concatenate([c, c_ctx[None, :], jnp.zeros((n_rows - B - 1, D), F32)], axis=0)
    mod = ada_mod(c_rows, w_ada, b_ada).reshape(depth, n_rows, N_MOD, 1, D)
    mod_lat = mod[:, :B]
    mod_ctx = mod[:, B:B + 1]

    cos_l, sin_l = _rope_tables(S)
    cos_c, sin_c = jnp.ones((B * Tc, HEAD_DIM), F32), jnp.zeros((B * Tc, HEAD_DIM), F32)
    q_fold = HEAD_DIM ** -0.5 * math.log2(math.e)
    gain_qk = jnp.concatenate([jnp.tile(q_gain * q_fold, (1, attn_w // HEAD_DIM)),
                               jnp.tile(k_gain, (1, kv_w // HEAD_DIM))], axis=1)
    gw = fnet_w // FNET_GROUPS
    cg, sg = _dft_tables(gw)
    cs_chan = jnp.concatenate([cg, sg], axis=1).astype(BF)

    tm_ffn = min(2048, S)
    tm_l = min(1024, S)
    tm_c = B * Tc
    xl = x
    xc = ctx.reshape(1, B * Tc, D)
    mods = {"lat": mod_lat, "ctx": mod_ctx}

    wts = {}
    for l in range(depth):
        wts["branch", l] = (w_branch, (l,))
        wts["out", l] = (w_out, (l,))
        for i in range(2):
            wts["ff_in", l, i] = (w_ff_in, (l, i))
            wts["ff_out", l, i] = (w_ff_out, (l, i))
    w_branch_rows = w_branch.reshape(depth, N_BRANCH * w_branch.shape[2], D)

    def ffn(xs, which, l, i, k_mod):
        lat = which == "lat"
        mod_l = mods[which][l]
        h = norm_mod(xs, g_norm[l, 2 * i], mod_l, k_mod, k_mod + 1)
        act = swiglu_in(h, *wts["ff_in", l, i], tm_ffn if lat else tm_c)
        return resid_out(act, *wts["ff_out", l, i], xs, mod_l, k_mod + 2, 0.5, tm_l if lat else tm_c)

    for l in range(depth):
        last = l == depth - 1
        xl = ffn(xl, "lat", l, 0, 0)
        xc = ffn(xc, "ctx", l, 0, 0)

        hl = norm_mod(xl, g_norm[l, 1], mod_lat[l], 3, 4)
        hc = norm_mod(xc, g_norm[l, 1], mod_ctx[l], 3, 4)
        gain = gain_qk[l:l + 1]
        bias_g = b_gate[l].reshape(1, N_BRANCH * D)

        qk_c = proj_qk(hc, w_in, l, attn_w + kv_w, gain, cos_c, sin_c, B * Tc, tm_c).reshape(B, Tc, -1)
        v_c = proj(hc, w_in, l, v0, kv_w, BF, tm_c).reshape(B, Tc, kv_w)
        xs_c = proj(hc, w_in, l, x0, lru_w, F32, tm_c).reshape(B, Tc, lru_w)
        if last:
            gy_c = jnp.zeros((B, Tc, lru_w), F32)
        else:
            gy_c = proj(hc, w_in, l, y0, lru_w, F32, tm_c, act="gelu").reshape(B, Tc, lru_w)
        lru_args = (conv_w, conv_b, lru_wa_b, lru_ba, lru_wx_b, lru_bx, lru_lam, l)
        lru_c, h_fin = lru_mix(xs_c, gy_c, jnp.zeros((B, 2, lru_w), F32), *lru_args)

        qk_l = proj_qk(hl, w_in, l, attn_w + kv_w, gain, cos_l, sin_l, S, tm_l)
        v_l = proj(hl, w_in, l, v0, kv_w, BF, tm_l)
        xs_l = proj(hl, w_in, l, x0, lru_w, F32, tm_l)
        gy_l = proj(hl, w_in, l, y0, lru_w, F32, tm_l, act="gelu")
        f_l = proj(hl, w_in, l, f0, fnet_w, BF, tm_l)
        gate_l = proj(hl, w_in, l, g0, N_BRANCH * D, F32, tm_l, act="sigmoid_bias", bias=bias_g)

        cast_keys = [("branch", l), ("out", l), ("ff_in", l, 1), ("ff_out", l, 1)]
        if not last:
            cast_keys += [("ff_in", l + 1, 0), ("ff_out", l + 1, 0)]
        cast = [(w_branch_rows, (l,)) if key[0] == "branch" else wts[key] for key in cast_keys]
        attn_l, copies = attention(qk_l, v_l, qk_c, v_c, attn_w, True, cast=cast)
        for key, w_bf in zip(cast_keys, copies):
            if key[0] == "branch":
                w_bf = w_bf.reshape(w_branch.shape[1:])
            wts[key] = (w_bf, ())

        lru_l, _ = lru_mix(xs_l, gy_l, h_fin, *lru_args)
        four_l = fourier_branch(f_l, cs_chan, tm_l, B, S, 1.0 / math.sqrt(S * gw))
        merged = gated_merge((attn_l, lru_l, four_l), *wts["branch", l], gate_l, tm_l)
        xl = resid_out(merged, *wts["out", l], xl, mod_lat[l], 5, 1.0, tm_l)
        xl = ffn(xl, "lat", l, 1, 6)

        if not last:
            f_c = proj(hc, w_in, l, f0, fnet_w, BF, tm_c)
            gate_c = proj(hc, w_in, l, g0, N_BRANCH * D, F32, tm_c, act="sigmoid_bias", bias=bias_g)
            attn_c, _ = attention(qk_c, v_c, qk_c, v_c, attn_w, False)
            four_c = fourier_branch(f_c, cs_chan, tm_c, B, Tc, 1.0 / math.sqrt(Tc * gw))
            merged_c = gated_merge((attn_c.reshape(1, B * Tc, attn_w), lru_c.reshape(1, B * Tc, lru_w),
                                    four_c.reshape(1, B * Tc, fnet_w)), *wts["branch", l], gate_c, tm_c)
            xc = resid_out(merged_c, *wts["out", l], xc, mod_ctx[l], 5, 1.0, tm_c)
            xc = ffn(xc, "ctx", l, 1, 6)

    return final_norm(xl, g_final)
```

```python
import functools
import math

import jax
import jax.numpy as jnp
from jax import lax
from jax.experimental import pallas as pl
from jax.experimental.pallas import tpu as pltpu

F32 = jnp.float32
BF = jnp.bfloat16

VMEM_LIMIT_BYTES = 56 * 1024 * 1024
LANES = 128
SUBLANES = 8
BF16_SUBLANES = 16

HEAD_DIM = 128
Q_GROUP = 4
GRID_W = 64
CONV_W = 4
CONV_LEFT = 2
LRU_C = 8.0
LRU_BLOCK = 128
FNET_GROUPS = 4
N_BRANCH = 3
N_MOD = 9
ROPE_THETA = 10000.0
EPS = 1e-6
CONV_PAD = 8


def _cparams(*sem):
    return pltpu.CompilerParams(dimension_semantics=sem, vmem_limit_bytes=VMEM_LIMIT_BYTES)


def _silu(a):
    return a * jax.nn.sigmoid(a)


MATMUL_VMEM_BUDGET = 55 * 1024 * 1024


def _plan_tiles(tm_max, K, w_dtype, w_tiles, io_bytes_per_elem, tmp_bytes_per_elem, extra_bytes_per_row=0,
                tn_options=(512, 256)):
    w_bytes = jnp.dtype(w_dtype).itemsize
    w_copy_bytes = 0 if w_dtype == BF else 2
    tm = tm_max
    while tm >= 256:
        for tn in tn_options:
            need = (2 * tm * K * 2 + w_tiles * K * tn * (2 * w_bytes + w_copy_bytes)
                    + tm * tn * (2 * io_bytes_per_elem + tmp_bytes_per_elem) + tm * extra_bytes_per_row)
            if need <= MATMUL_VMEM_BUDGET:
                return tm, tn
        tm //= 2
    raise ValueError("no matmul tiling fits VMEM")


def _lhs_spec(tm, K):
    return pl.BlockSpec((None, tm, K), lambda g, m, n: (g, m, 0))


def _dot(a, w):
    return jnp.dot(a, w.astype(BF), preferred_element_type=F32)


def _ada_kernel(c_ref, w_ref, b_ref, o_ref):
    s = _silu(c_ref[...]).astype(BF)
    o_ref[...] = _dot(s, w_ref[...]) + b_ref[...]


def ada_mod(c_rows, w_ada, b_ada, tn=1024):
    depth, d, n = w_ada.shape
    rows = c_rows.shape[0]
    return pl.pallas_call(
        _ada_kernel,
        out_shape=jax.ShapeDtypeStruct((depth, rows, n), F32),
        grid=(depth, n // tn),
        in_specs=[pl.BlockSpec((rows, d), lambda l, j: (0, 0)),
                  pl.BlockSpec((None, d, tn), lambda l, j: (l, 0, j)),
                  pl.BlockSpec((None, 1, tn), lambda l, j: (l, 0, j))],
        out_specs=pl.BlockSpec((None, rows, tn), lambda l, j: (l, 0, j)),
        compiler_params=_cparams("arbitrary", "arbitrary"),
        name="ada_mod",
    )(c_rows, w_ada, b_ada.reshape(depth, 1, n))


def _norm_mod_kernel(x_ref, g_ref, sh_ref, sc_ref, o_ref):
    x = x_ref[...]
    ms = jnp.mean(x * x, axis=-1, keepdims=True)
    y = x_ref[...] * lax.rsqrt(ms + EPS) * g_ref[...]
    o_ref[...] = (y * (1.0 + sc_ref[...]) + sh_ref[...]).astype(o_ref.dtype)


def norm_mod(x, g, mod, k_shift, k_scale, ts=512):
    G, S, D = x.shape
    ts = min(ts, S)
    return pl.pallas_call(
        _norm_mod_kernel,
        out_shape=jax.ShapeDtypeStruct((G, S, D), BF),
        grid=(G, S // ts),
        in_specs=[pl.BlockSpec((None, ts, D), lambda g_, i: (g_, i, 0)),
                  pl.BlockSpec((1, D), lambda g_, i: (0, 0)),
                  pl.BlockSpec((None, None, 1, D), lambda g_, i: (g_, k_shift, 0, 0)),
                  pl.BlockSpec((None, None, 1, D), lambda g_, i: (g_, k_scale, 0, 0))],
        out_specs=pl.BlockSpec((None, ts, D), lambda g_, i: (g_, i, 0)),
        compiler_params=_cparams("arbitrary", "arbitrary"),
        name="norm_mod",
    )(x, g.reshape(1, D), mod, mod)


def _final_norm_kernel(x_ref, g_ref, o_ref):
    x = x_ref[...]
    ms = jnp.mean(x * x, axis=-1, keepdims=True)
    o_ref[...] = x * lax.rsqrt(ms + EPS) * g_ref[...]


def final_norm(x, g, ts=512):
    G, S, D = x.shape
    ts = min(ts, S)
    return pl.pallas_call(
        _final_norm_kernel,
        out_shape=jax.ShapeDtypeStruct((G, S, D), F32),
        grid=(G, S // ts),
        in_specs=[pl.BlockSpec((None, ts, D), lambda g_, i: (g_, i, 0)),
                  pl.BlockSpec((1, D), lambda g_, i: (0, 0))],
        out_specs=pl.BlockSpec((None, ts, D), lambda g_, i: (g_, i, 0)),
        compiler_params=_cparams("arbitrary", "arbitrary"),
        name="final_norm",
    )(x, g.reshape(1, D))


def _swiglu_kernel(h_ref, wa_ref, wb_ref, o_ref):
    h = h_ref[...]
    a = _dot(h, wa_ref[...])
    b = _dot(h, wb_ref[...])
    o_ref[...] = (_silu(a) * b).astype(o_ref.dtype)


def _w_spec(w_idx, block, col_block):
    lead = tuple(w_idx)
    return pl.BlockSpec((None,) * len(lead) + block,
                        lambda g, m, n: lead + (0,) * (len(block) - 1) + (col_block(n),))


def swiglu_in(h, w, w_idx, tm):
    G, S, K = h.shape
    F = w.shape[-1] // 2
    tm, tn = _plan_tiles(tm, K, w.dtype, 2, 2, 0, tn_options=(256,))
    nb = F // tn
    return pl.pallas_call(
        _swiglu_kernel,
        out_shape=jax.ShapeDtypeStruct((G, S, F), BF),
        grid=(G, S // tm, nb),
        in_specs=[_lhs_spec(tm, K),
                  _w_spec(w_idx, (K, tn), lambda n: n),
                  _w_spec(w_idx, (K, tn), lambda n: n + nb)],
        out_specs=pl.BlockSpec((None, tm, tn), lambda g, m, n: (g, m, n)),
        compiler_params=_cparams("arbitrary", "arbitrary", "arbitrary"),
        name="swiglu_in",
    )(h, w, w)


def _resid_kernel(coef, a_ref, w_ref, x_ref, g_ref, o_ref):
    y = _dot(a_ref[...], w_ref[...])
    o_ref[...] = x_ref[...] + (coef * g_ref[...]) * y


def resid_out(a, w, w_idx, x, mod, k_gate, coef, tm):
    G, S, K = a.shape
    D = x.shape[-1]
    tm, tn = _plan_tiles(tm, K, w.dtype, 1, 8, 16)
    return pl.pallas_call(
        functools.partial(_resid_kernel, coef),
        out_shape=jax.ShapeDtypeStruct((G, S, D), F32),
        grid=(G, S // tm, D // tn),
        in_specs=[_lhs_spec(tm, K),
                  _w_spec(w_idx, (K, tn), lambda n: n),
                  pl.BlockSpec((None, tm, tn), lambda g, m, n: (g, m, n)),
                  pl.BlockSpec((None, None, 1, tn), lambda g, m, n: (g, k_gate, 0, n))],
        out_specs=pl.BlockSpec((None, tm, tn), lambda g, m, n: (g, m, n)),
        compiler_params=_cparams("arbitrary", "arbitrary", "arbitrary"),
        name="resid_out",
    )(a, w, x, mod)


def _proj_kernel(act, h_ref, w_ref, *rest):
    o_ref = rest[-1]
    acc = _dot(h_ref[...], w_ref[...])
    if act == "gelu":
        acc = jax.nn.gelu(acc)
    elif act == "sigmoid_bias":
        acc = jax.nn.sigmoid(acc + rest[0][...])
    o_ref[...] = acc.astype(o_ref.dtype)


def proj(h, w_in, l, col0, ncols, out_dtype, tm, act="none", bias=None):
    G, S, K = h.shape
    tm, tn = _plan_tiles(tm, K, w_in.dtype, 1, jnp.dtype(out_dtype).itemsize, 8)
    cb = col0 // tn
    in_specs = [_lhs_spec(tm, K),
                pl.BlockSpec((None, K, tn), lambda g, m, n: (l, 0, cb + n))]
    args = [h, w_in]
    if bias is not None:
        in_specs.append(pl.BlockSpec((1, tn), lambda g, m, n: (0, n)))
        args.append(bias)
    return pl.pallas_call(
        functools.partial(_proj_kernel, act),
        out_shape=jax.ShapeDtypeStruct((G, S, ncols), out_dtype),
        grid=(G, S // tm, ncols // tn),
        in_specs=in_specs,
        out_specs=pl.BlockSpec((None, tm, tn), lambda g, m, n: (g, m, n)),
        compiler_params=_cparams("arbitrary", "arbitrary", "arbitrary"),
        name="proj_" + act,
    )(*args)


def _qk_kernel(h_ref, w_ref, gain_ref, cos_ref, sin_ref, swap_ref, o_ref):
    acc = _dot(h_ref[...], w_ref[...])
    cos = cos_ref[...]
    sin = sin_ref[...]
    swap = swap_ref[...]
    for hd in range(acc.shape[1] // HEAD_DIM):
        cols = slice(hd * HEAD_DIM, (hd + 1) * HEAD_DIM)
        v = acc[:, cols]
        ms = jnp.mean(v * v, axis=-1, keepdims=True)
        y = v * lax.rsqrt(ms + EPS) * gain_ref[:, cols]
        hi = y.astype(BF)
        lo = (y - hi.astype(F32)).astype(BF)
        partner = (jnp.dot(hi, swap, preferred_element_type=F32)
                   + jnp.dot(lo, swap, preferred_element_type=F32))
        o_ref[:, cols] = (y * cos + partner * sin).astype(o_ref.dtype)


def proj_qk(h, w_in, l, ncols, gain, cos, sin, rows_per_seq, tm):
    G, S, K = h.shape
    tm, tn = _plan_tiles(tm, K, w_in.dtype, 1, 2, 8, extra_bytes_per_row=2 * 2 * HEAD_DIM * 4)
    seq_blocks = rows_per_seq // tm
    lane = jnp.arange(HEAD_DIM)
    quarter = HEAD_DIM // 4
    partner_lane = jnp.where((lane % (2 * quarter)) < quarter, lane + quarter, lane - quarter)
    swap = (lane[:, None] == partner_lane[None, :]).astype(BF)
    return pl.pallas_call(
        _qk_kernel,
        out_shape=jax.ShapeDtypeStruct((G, S, ncols), BF),
        grid=(G, S // tm, ncols // tn),
        in_specs=[_lhs_spec(tm, K),
                  pl.BlockSpec((None, K, tn), lambda g, m, n: (l, 0, n)),
                  pl.BlockSpec((1, tn), lambda g, m, n: (0, n)),
                  pl.BlockSpec((tm, HEAD_DIM), lambda g, m, n: (m % seq_blocks, 0)),
                  pl.BlockSpec((tm, HEAD_DIM), lambda g, m, n: (m % seq_blocks, 0)),
                  pl.BlockSpec((HEAD_DIM, HEAD_DIM), lambda g, m, n: (0, 0))],
        out_specs=pl.BlockSpec((None, tm, tn), lambda g, m, n: (g, m, n)),
        compiler_params=_cparams("arbitrary", "arbitrary", "arbitrary"),
        name="proj_qk",
    )(h, w_in, gain, cos, sin, swap)


def _attn_kernel(with_latent_keys, n_cast, q_ref, kc_ref, vc_ref, *rest):
    n_in = 2 if with_latent_keys else 0
    cast_in = rest[n_in:n_in + n_cast]
    o_ref = rest[n_in + n_cast]
    cast_out = rest[n_in + n_cast + 1:]
    for w_src, w_dst in zip(cast_in, cast_out):
        w_dst[...] = w_src[...].astype(w_dst.dtype)
    if with_latent_keys:
        k = jnp.concatenate([kc_ref[...], rest[0][...]], axis=0)
        v = jnp.concatenate([vc_ref[...], rest[1][...]], axis=0)
    else:
        k, v = kc_ref[...], vc_ref[...]
    v_ones = jnp.concatenate([v, jnp.ones_like(v)], axis=1)
    for g in range(Q_GROUP):
        cols = slice(g * HEAD_DIM, (g + 1) * HEAD_DIM)
        s = lax.dot_general(q_ref[:, cols], k, (((1,), (1,)), ((), ())), preferred_element_type=F32)
        p = jnp.exp2(s - jnp.max(s, axis=-1, keepdims=True))
        acc = jnp.dot(p.astype(BF), v_ones, preferred_element_type=F32)
        o_ref[:, cols] = (acc[:, :HEAD_DIM] / acc[:, HEAD_DIM:]).astype(o_ref.dtype)


def attention(qk, v, qk_ctx, v_ctx, attn_width, with_latent_keys, cast=(), tq=512):
    B, S, _ = qk.shape
    Tc = qk_ctx.shape[1]
    n_kv = v.shape[-1] // HEAD_DIM
    gw = Q_GROUP * HEAD_DIM
    kcol = attn_width // HEAD_DIM
    tq = min(tq, S)
    n_q = S // tq
    n_steps = B * n_kv * n_q
    in_specs = [pl.BlockSpec((None, tq, gw), lambda b, h, i: (b, i, h)),
                pl.BlockSpec((None, Tc, HEAD_DIM), lambda b, h, i: (b, 0, kcol + h)),
                pl.BlockSpec((None, Tc, HEAD_DIM), lambda b, h, i: (b, 0, h))]
    args = [qk, qk_ctx, v_ctx]
    if with_latent_keys:
        in_specs += [pl.BlockSpec((None, S, HEAD_DIM), lambda b, h, i: (b, 0, kcol + h)),
                     pl.BlockSpec((None, S, HEAD_DIM), lambda b, h, i: (b, 0, h))]
        args += [qk, v]
    out_shape = [jax.ShapeDtypeStruct((B, S, attn_width), BF)]
    out_specs = [pl.BlockSpec((None, tq, gw), lambda b, h, i: (b, i, h))]

    def step(b, h, i):
        return (b * n_kv + h) * n_q + i

    for w, w_idx in cast:
        rows, cols = w.shape[-2:]
        slab = rows // n_steps
        assert slab * n_steps == rows and slab % BF16_SUBLANES == 0, (rows, n_steps)
        lead = tuple(w_idx)
        in_specs.append(pl.BlockSpec((None,) * len(lead) + (slab, cols),
                                     lambda b, h, i, lead=lead: lead + (step(b, h, i), 0)))
        args.append(w)
        out_shape.append(jax.ShapeDtypeStruct((rows, cols), BF))
        out_specs.append(pl.BlockSpec((slab, cols), lambda b, h, i: (step(b, h, i), 0)))
    outs = pl.pallas_call(
        functools.partial(_attn_kernel, with_latent_keys, len(cast)),
        out_shape=tuple(out_shape),
        grid=(B, n_kv, n_q),
        in_specs=in_specs,
        out_specs=tuple(out_specs),
        compiler_params=_cparams("arbitrary", "arbitrary", "arbitrary"),
        name="attention",
    )(*args)
    return outs[0], list(outs[1:])


def _one_minus_sq(a, y):
    poly = y * (1.0 / 120.0) + (1.0 / 24.0)
    for coef in (1.0 / 6.0, 0.5, 1.0):
        poly = poly * y + coef
    return jnp.where(y > -0.0625, -(poly * y), 1.0 - a * a)


def _tile_scan(a, b, reverse):
    row = lax.broadcasted_iota(jnp.int32, a.shape, 0)
    for d in (1, 2, 4):
        shift = SUBLANES - d if reverse else d
        valid = (row < SUBLANES - d) if reverse else (row >= d)
        a_prev = jnp.where(valid, pltpu.roll(a, shift, 0), 1.0)
        b_prev = jnp.where(valid, pltpu.roll(b, shift, 0), 0.0)
        b = a * b_prev + b
        a = a * a_prev
    return a, b


def _lru_kernel(S, tchunk, unroll, x_ref, gy_ref, h0_ref, cw_ref, cb_ref, wa_ref, ba_ref, wx_ref, bx_ref,
                lam_ref, o_ref, hfin_ref, xs_ref, a_ref, b_ref, h_ref):
    zeros = jnp.zeros((CONV_PAD, LRU_BLOCK), F32)
    xs_ref[0:CONV_PAD, :] = zeros
    xs_ref[CONV_PAD + S:2 * CONV_PAD + S, :] = zeros
    xs_ref[CONV_PAD:CONV_PAD + S, :] = x_ref[...]

    neg_lam = -lam_ref[...]
    softplus = jnp.maximum(neg_lam, 0.0) + jnp.log1p(jnp.exp(-jnp.abs(neg_lam)))
    log_a_per_r = -LRU_C * softplus
    log2_a_per_r = log_a_per_r * math.log2(math.e)

    for c in range(S // tchunk):
        t0 = c * tchunk
        xc = cb_ref[...]
        for j in range(CONV_W):
            xc = xc + xs_ref[pl.ds(CONV_PAD + t0 + j - CONV_LEFT, tchunk), :] * cw_ref[j:j + 1, :]
        xb = xc.astype(BF)
        for d in range(2):
            r = jax.nn.sigmoid(jnp.dot(xb, wa_ref[d], preferred_element_type=F32) + ba_ref[d:d + 1, :])
            i = jax.nn.sigmoid(jnp.dot(xb, wx_ref[d], preferred_element_type=F32) + bx_ref[d:d + 1, :])
            a = jnp.exp2(r * log2_a_per_r[d:d + 1, :])
            a_ref[d, pl.ds(t0, tchunk), :] = a
            one_minus_a2 = _one_minus_sq(a, r * (2.0 * log_a_per_r[d:d + 1, :]))
            b_ref[d, pl.ds(t0, tchunk), :] = jnp.sqrt(one_minus_a2) * (i * xc)

    n_tiles = S // SUBLANES

    def body(i, carry):
        hf, hr = carry
        tf = pl.multiple_of(i * SUBLANES, SUBLANES)
        af, bf = _tile_scan(a_ref[0, pl.ds(tf, SUBLANES), :], b_ref[0, pl.ds(tf, SUBLANES), :], False)
        h = bf + af * hf
        h_ref[0, pl.ds(tf, SUBLANES), :] = h
        hf = h[SUBLANES - 1:SUBLANES, :]
        tr = pl.multiple_of((n_tiles - 1 - i) * SUBLANES, SUBLANES)
        ar, br = _tile_scan(a_ref[1, pl.ds(tr, SUBLANES), :], b_ref[1, pl.ds(tr, SUBLANES), :], True)
        h = br + ar * hr
        h_ref[1, pl.ds(tr, SUBLANES), :] = h
        hr = h[0:1, :]
        return hf, hr

    hf, hr = lax.fori_loop(0, n_tiles, body, (h0_ref[0:1, :], h0_ref[1:2, :]), unroll=unroll)
    hfin_ref[0:1, :] = hf
    hfin_ref[1:2, :] = hr
    o_ref[...] = ((h_ref[0] + h_ref[1]) * gy_ref[...]).astype(o_ref.dtype)


def lru_mix(xseg, gy, h0, conv_w, conv_b, wa, ba, wx, bx, lam, l, tchunk=512, unroll=4):
    B, S, W = xseg.shape
    tchunk = min(tchunk, S)
    nblk = W // LRU_BLOCK
    seq = pl.BlockSpec((None, S, LRU_BLOCK), lambda b, n: (b, 0, n))
    st = pl.BlockSpec((None, 2, LRU_BLOCK), lambda b, n: (b, 0, n))
    vec2 = pl.BlockSpec((None, 2, LRU_BLOCK), lambda b, n: (l, 0, n))
    wblk = pl.BlockSpec((None, 2, None, LRU_BLOCK, LRU_BLOCK), lambda b, n: (l, 0, n, 0, 0))
    return pl.pallas_call(
        functools.partial(_lru_kernel, S, tchunk, unroll),
        out_shape=(jax.ShapeDtypeStruct((B, S, W), BF), jax.ShapeDtypeStruct((B, 2, W), F32)),
        grid=(B, nblk),
        in_specs=[seq, seq, st,
                  pl.BlockSpec((None, CONV_W, LRU_BLOCK), lambda b, n: (l, 0, n)),
                  pl.BlockSpec((None, 1, LRU_BLOCK), lambda b, n: (l, 0, n)),
                  wblk, vec2, wblk, vec2, vec2],
        out_specs=(seq, st),
        scratch_shapes=[pltpu.VMEM((S + 2 * CONV_PAD, LRU_BLOCK), F32),
                        pltpu.VMEM((2, S, LRU_BLOCK), F32),
                        pltpu.VMEM((2, S, LRU_BLOCK), F32),
                        pltpu.VMEM((2, S, LRU_BLOCK), F32)],
        compiler_params=_cparams("arbitrary", "arbitrary"),
        name="lru_mix",
    )(xseg, gy, h0, conv_w, conv_b.reshape(conv_b.shape[0], 1, W), wa, ba, wx, bx, lam)


def _dft_tables(n):
    k = jnp.arange(n, dtype=jnp.int32)
    ang = ((k[:, None] * k[None, :]) % n).astype(F32) * (2.0 * math.pi / n)
    return jnp.cos(ang), jnp.sin(ang)


def _chan_dft_kernel(f_ref, cs_ref, zc_ref, zs_ref):
    w = zc_ref.shape[-1]
    z = jnp.dot(f_ref[...], cs_ref[...], preferred_element_type=F32)
    zc_ref[...] = z[:, :w].astype(zc_ref.dtype)
    zs_ref[...] = z[:, w:].astype(zs_ref.dtype)


def chan_dft(f, cs, tm, out_dtype):
    G, S, W = f.shape
    gw = W // FNET_GROUPS
    blk = pl.BlockSpec((None, tm, gw), lambda g, m, n: (g, m, n))
    return pl.pallas_call(
        _chan_dft_kernel,
        out_shape=(jax.ShapeDtypeStruct((G, S, W), out_dtype),) * 2,
        grid=(G, S // tm, FNET_GROUPS),
        in_specs=[blk, pl.BlockSpec((gw, 2 * gw), lambda g, m, n: (0, 0))],
        out_specs=(blk, blk),
        compiler_params=_cparams("arbitrary", "arbitrary", "arbitrary"),
        name="chan_dft",
    )(f, cs)


def _time_dft_kernel(scale, c_ref, s_ref, zc_ref, zs_ref, o_ref):
    re = jnp.dot(c_ref[...], zc_ref[...], preferred_element_type=F32)
    re = re - jnp.dot(s_ref[...], zs_ref[...], preferred_element_type=F32)
    o_ref[...] = (re * scale).astype(o_ref.dtype)


def time_dft_dense(zc, zs, scale, tm=512, tn=512):
    B, T, W = zc.shape
    tm = min(tm, T)
    ct, st = (t.astype(BF) for t in _dft_tables(T))
    a_spec = pl.BlockSpec((tm, T), lambda b, n, m: (m, 0))
    z_spec = pl.BlockSpec((None, T, tn), lambda b, n, m: (b, 0, n))
    return pl.pallas_call(
        functools.partial(_time_dft_kernel, scale),
        out_shape=jax.ShapeDtypeStruct((B, T, W), BF),
        grid=(B, W // tn, T // tm),
        in_specs=[a_spec, a_spec, z_spec, z_spec],
        out_specs=pl.BlockSpec((None, tm, tn), lambda b, n, m: (b, m, n)),
        compiler_params=_cparams("arbitrary", "arbitrary", "arbitrary"),
        name="time_dft",
    )(ct, st, zc, zs)


def _dft_stage1_kernel(zc_ref, zs_ref, rot_ref, twc_ref, tws_ref, oc_ref, os_ref):
    n1 = zc_ref.shape[0]
    tn = zc_ref.shape[-1]
    rot = rot_ref[...]
    for j in range(zc_ref.shape[1]):
        z = jnp.concatenate([zc_ref[:, j, :], zs_ref[:, j, :]], axis=0).astype(BF)
        res = jnp.dot(rot, z, preferred_element_type=F32)
        ac, as_ = res[:n1], res[n1:]
        twc = jnp.tile(twc_ref[j], (1, tn // LANES))
        tws = jnp.tile(tws_ref[j], (1, tn // LANES))
        oc_ref[j] = (ac * twc - as_ * tws).astype(oc_ref.dtype)
        os_ref[j] = (as_ * twc + ac * tws).astype(os_ref.dtype)


def dft_stage1(zc, zs, n1, n2, tn=512):
    B, T, W = zc.shape
    c, s = _dft_tables(n1)
    rot = jnp.concatenate([jnp.concatenate([c, -s], axis=1), jnp.concatenate([s, c], axis=1)], axis=0).astype(BF)
    cc = jnp.arange(n1, dtype=jnp.int32)
    bb = jnp.arange(n2, dtype=jnp.int32)
    ang = (bb[:, None] * cc[None, :]).astype(F32) * (2.0 * math.pi / T)
    twc = jnp.broadcast_to(jnp.cos(ang)[:, :, None], (n2, n1, LANES))
    tws = jnp.broadcast_to(jnp.sin(ang)[:, :, None], (n2, n1, LANES))
    z_spec = pl.BlockSpec((None, n1, SUBLANES, tn), lambda b, i, n: (b, 0, i, n))
    tw_spec = pl.BlockSpec((SUBLANES, n1, LANES), lambda b, i, n: (i, 0, 0))
    o_spec = pl.BlockSpec((None, SUBLANES, n1, tn), lambda b, i, n: (b, i, 0, n))
    return pl.pallas_call(
        _dft_stage1_kernel,
        out_shape=(jax.ShapeDtypeStruct((B, n2, n1, W), BF),) * 2,
        grid=(B, n2 // SUBLANES, W // tn),
        in_specs=[z_spec, z_spec, pl.BlockSpec((2 * n1, 2 * n1), lambda b, i, n: (0, 0)), tw_spec, tw_spec],
        out_specs=(o_spec, o_spec),
        compiler_params=_cparams("arbitrary", "arbitrary", "arbitrary"),
        name="dft_stage1",
    )(zc.reshape(B, n1, n2, W), zs.reshape(B, n1, n2, W), rot, twc, tws)


def _dft_stage2_kernel(scale, ac_ref, as_ref, cs_ref, o_ref, ac_f32, as_f32, o_f32):
    cs = cs_ref[...]
    ac_f32[...] = ac_ref[...].astype(F32)
    as_f32[...] = as_ref[...].astype(F32)
    for j in range(ac_ref.shape[1]):
        a = jnp.concatenate([ac_f32[:, j, :], as_f32[:, j, :]], axis=0).astype(BF)
        o_f32[:, j, :] = jnp.dot(cs, a, preferred_element_type=F32) * scale
    o_ref[...] = o_f32[...].astype(o_ref.dtype)


def dft_stage2(ac, as_, scale, tn=512):
    B, n2, n1, W = ac.shape
    c, s = _dft_tables(n2)
    cs = jnp.concatenate([c, -s], axis=1).astype(BF)
    cblk = BF16_SUBLANES
    a_spec = pl.BlockSpec((None, n2, cblk, tn), lambda b, i, n: (b, 0, i, n))
    out = pl.pallas_call(
        functools.partial(_dft_stage2_kernel, scale),
        out_shape=jax.ShapeDtypeStruct((B, n2, n1, W), BF),
        grid=(B, n1 // cblk, W // tn),
        in_specs=[a_spec, a_spec, pl.BlockSpec((n2, 2 * n2), lambda b, i, n: (0, 0))],
        out_specs=a_spec,
        scratch_shapes=[pltpu.VMEM((n2, cblk, tn), F32)] * 3,
        compiler_params=_cparams("arbitrary", "arbitrary", "arbitrary"),
        name="dft_stage2",
    )(ac, as_, cs)
    return out.reshape(B, n2 * n1, W)


def _two_stage_dft_ok(T):
    n = math.isqrt(T)
    return n * n == T and n % BF16_SUBLANES == 0


def fourier_branch(f, cs_chan, tm, B, T, scale):
    W = f.shape[-1]
    if _two_stage_dft_ok(T):
        n = math.isqrt(T)
        zc, zs = chan_dft(f, cs_chan, tm, F32)
        ac, as_ = dft_stage1(zc.reshape(B, T, W), zs.reshape(B, T, W), n, n)
        return dft_stage2(ac, as_, scale)
    zc, zs = chan_dft(f, cs_chan, tm, BF)
    return time_dft_dense(zc.reshape(B, T, W), zs.reshape(B, T, W), scale)


def _merge_kernel(b0_ref, b1_ref, b2_ref, w_ref, g0_ref, g1_ref, g2_ref, o_ref):
    acc = g0_ref[...] * _dot(b0_ref[...], w_ref[0])
    acc = acc + g1_ref[...] * _dot(b1_ref[...], w_ref[1])
    acc = acc + g2_ref[...] * _dot(b2_ref[...], w_ref[2])
    o_ref[...] = acc.astype(o_ref.dtype)


def gated_merge(branches, w_branch, w_idx, gates, tm):
    G, S, K = branches[0].shape
    tm = min(tm, 1024)
    tn = 256
    D = w_branch.shape[-1]
    nb = D // tn
    b_spec = pl.BlockSpec((None, tm, K), lambda g, m, n: (g, m, 0))

    def g_spec(j):
        return pl.BlockSpec((None, tm, tn), lambda g, m, n: (g, m, j * nb + n))

    return pl.pallas_call(
        _merge_kernel,
        out_shape=jax.ShapeDtypeStruct((G, S, D), BF),
        grid=(G, S // tm, nb),
        in_specs=[b_spec, b_spec, b_spec,
                  _w_spec(w_idx, (N_BRANCH, K, tn), lambda n: n),
                  g_spec(0), g_spec(1), g_spec(2)],
        out_specs=pl.BlockSpec((None, tm, tn), lambda g, m, n: (g, m, n)),
        compiler_params=_cparams("arbitrary", "arbitrary", "arbitrary"),
        name="gated_merge",
    )(*branches, w_branch, gates, gates, gates)


def _rope_tables(n_tokens):
    rows = n_tokens // GRID_W
    row = jnp.repeat(jnp.arange(rows, dtype=F32), GRID_W)
    col = jnp.tile(jnp.arange(GRID_W, dtype=F32), rows)
    axis_dim = HEAD_DIM // 2
    inv_freq = ROPE_THETA ** (-jnp.arange(0, axis_dim, 2, dtype=F32) / axis_dim)
    ar, ac = row[:, None] * inv_freq, col[:, None] * inv_freq
    cos = jnp.concatenate([jnp.cos(ar), jnp.cos(ar), jnp.cos(ac), jnp.cos(ac)], axis=-1)
    sin = jnp.concatenate([-jnp.sin(ar), jnp.sin(ar), -jnp.sin(ac), jnp.sin(ac)], axis=-1)
    return cos, sin


def kernel(x, c, ctx, c_ctx, w_ada, b_ada, g_norm, w_ff_in, w_ff_out, w_in, b_gate, q_gain, k_gain,
           conv_w, conv_b, lru_wa, lru_ba, lru_wx, lru_bx, lru_lam, w_branch, w_out, g_final):
    B, S, D = x.shape
    Tc = ctx.shape[1]
    depth = w_ada.shape[0]
    lru_w = conv_w.shape[-1]
    fnet_w = D // 2
    attn_w = (D // 256) * HEAD_DIM
    kv_w = attn_w // Q_GROUP
    v0 = attn_w + kv_w
    x0 = v0 + kv_w
    y0 = x0 + lru_w
    f0 = y0 + lru_w
    g0 = f0 + fnet_w

    lru_wa_b = lru_wa.astype(BF)
    lru_wx_b = lru_wx.astype(BF)

    n_rows = SUBLANES
    c_rows = jnp.concatenate([c, c_ctx[None, :], jnp.zeros((n_rows - B - 1, D), F32)], axis=0)
    mod = ada_mod(c_rows, w_ada, b_ada).reshape(depth, n_rows, N_MOD, 1, D)
    mod_lat = mod[:, :B]
    mod_ctx = mod[:, B:B + 1]

    cos_l, sin_l = _rope_tables(S)
    cos_c, sin_c = jnp.ones((B * Tc, HEAD_DIM), F32), jnp.zeros((B * Tc, HEAD_DIM), F32)
    q_fold = HEAD_DIM ** -0.5 * math.log2(math.e)
    gain_qk = jnp.concatenate([jnp.tile(q_gain * q_fold, (1, attn_w // HEAD_DIM)),
                               jnp.tile(k_gain, (1, kv_w // HEAD_DIM))], axis=1)
    gw = fnet_w // FNET_GROUPS
    cg, sg = _dft_tables(gw)
    cs_chan = jnp.concatenate([cg, sg], axis=1).astype(BF)

    tm_ffn = min(2048, S)
    tm_l = min(1024, S)
    tm_c = B * Tc
    xl = x
    xc = ctx.reshape(1, B * Tc, D)
    mods = {"lat": mod_lat, "ctx": mod_ctx}

    wts = {}
    for l in range(depth):
        wts["branch", l] = (w_branch, (l,))
        wts["out", l] = (w_out, (l,))
        for i in range(2):
            wts["ff_in", l, i] = (w_ff_in, (l, i))
            wts["ff_out", l, i] = (w_ff_out, (l, i))
    w_branch_rows = w_branch.reshape(depth, N_BRANCH * w_branch.shape[2], D)

    def ffn(xs, which, l, i, k_mod):
        lat = which == "lat"
        mod_l = mods[which][l]
        h = norm_mod(xs, g_norm[l, 2 * i], mod_l, k_mod, k_mod + 1)
        act = swiglu_in(h, *wts["ff_in", l, i], tm_ffn if lat else tm_c)
        return resid_out(act, *wts["ff_out", l, i], xs, mod_l, k_mod + 2, 0.5, tm_l if lat else tm_c)

    for l in range(depth):
        last = l == depth - 1
        xl = ffn(xl, "lat", l, 0, 0)
        xc = ffn(xc, "ctx", l, 0, 0)

        hl = norm_mod(xl, g_norm[l, 1], mod_lat[l], 3, 4)
        hc = norm_mod(xc, g_norm[l, 1], mod_ctx[l], 3, 4)
        gain = gain_qk[l:l + 1]
        bias_g = b_gate[l].reshape(1, N_BRANCH * D)

        qk_c = proj_qk(hc, w_in, l, attn_w + kv_w, gain, cos_c, sin_c, B * Tc, tm_c).reshape(B, Tc, -1)
        v_c = proj(hc, w_in, l, v0, kv_w, BF, tm_c).reshape(B, Tc, kv_w)
        xs_c = proj(hc, w_in, l, x0, lru_w, F32, tm_c).reshape(B, Tc, lru_w)
        if last:
            gy_c = jnp.zeros((B, Tc, lru_w), F32)
        else:
            gy_c = proj(hc, w_in, l, y0, lru_w, F32, tm_c, act="gelu").reshape(B, Tc, lru_w)
        lru_args = (conv_w, conv_b, lru_wa_b, lru_ba, lru_wx_b, lru_bx, lru_lam, l)
        lru_c, h_fin = lru_mix(xs_c, gy_c, jnp.zeros((B, 2, lru_w), F32), *lru_args)

        qk_l = proj_qk(hl, w_in, l, attn_w + kv_w, gain, cos_l, sin_l, S, tm_l)
        v_l = proj(hl, w_in, l, v0, kv_w, BF, tm_l)
        xs_l = proj(hl, w_in, l, x0, lru_w, F32, tm_l)
        gy_l = proj(hl, w_in, l, y0, lru_w, F32, tm_l, act="gelu")
        f_l = proj(hl, w_in, l, f0, fnet_w, BF, tm_l)
        gate_l = proj(hl, w_in, l, g0, N_BRANCH * D, F32, tm_l, act="sigmoid_bias", bias=bias_g)

        cast_keys = [("branch", l), ("out", l), ("ff_in", l, 1), ("ff_out", l, 1)]
        if not last:
            cast_keys += [("ff_in", l + 1, 0), ("ff_out", l + 1, 0)]
        cast = [(w_branch_rows, (l,)) if key[0] == "branch" else wts[key] for key in cast_keys]
        attn_l, copies = attention(qk_l, v_l, qk_c, v_c, attn_w, True, cast=cast)
        for key, w_bf in zip(cast_keys, copies):
            if key[0] == "branch":
                w_bf = w_bf.reshape(w_branch.shape[1:])
            wts[key] = (w_bf, ())

        lru_l, _ = lru_mix(xs_l, gy_l, h_fin, *lru_args)
        four_l = fourier_branch(f_l, cs_chan, tm_l, B, S, 1.0 / math.sqrt(S * gw))
        merged = gated_merge((attn_l, lru_l, four_l), *wts["branch", l], gate_l, tm_l)
        xl = resid_out(merged, *wts["out", l], xl, mod_lat[l], 5, 1.0, tm_l)
        xl = ffn(xl, "lat", l, 1, 6)

        if not last:
            f_c = proj(hc, w_in, l, f0, fnet_w, BF, tm_c)
            gate_c = proj(hc, w_in, l, g0, N_BRANCH * D, F32, tm_c, act="sigmoid_bias", bias=bias_g)
            attn_c, _ = attention(qk_c, v_c, qk_c, v_c, attn_w, False)
            four_c = fourier_branch(f_c, cs_chan, tm_c, B, Tc, 1.0 / math.sqrt(Tc * gw))
            merged_c = gated_merge((attn_c.reshape(1, B * Tc, attn_w), lru_c.reshape(1, B * Tc, lru_w),
                                    four_c.reshape(1, B * Tc, fnet_w)), *wts["branch", l], gate_c, tm_c)
            xc = resid_out(merged_c, *wts["out", l], xc, mod_ctx[l], 5, 1.0, tm_c)
            xc = ffn(xc, "ctx", l, 1, 6)

    return final_norm(xl, g_final)
```

```python
import functools
import math

import jax
import jax.numpy as jnp
from jax import lax
from jax.experimental import pallas as pl
from jax.experimental.pallas import tpu as pltpu

F32 = jnp.float32
BF = jnp.bfloat16

VMEM_LIMIT_BYTES = 56 * 1024 * 1024
ATTENTION_VMEM_LIMIT_BYTES = 60 * 1024 * 1024
LANES = 128
SUBLANES = 8
BF16_SUBLANES = 16

HEAD_DIM = 128
Q_GROUP = 4
GRID_W = 64
CONV_W = 4
CONV_LEFT = 2
LRU_C = 8.0
LRU_BLOCK = 128
FNET_GROUPS = 4
N_BRANCH = 3
N_MOD = 9
ROPE_THETA = 10000.0
EPS = 1e-6
CONV_PAD = 8


def _cparams(*sem):
    return pltpu.CompilerParams(dimension_semantics=sem, vmem_limit_bytes=VMEM_LIMIT_BYTES)


def _silu(a):
    return a * jax.nn.sigmoid(a)


MATMUL_VMEM_BUDGET = 55 * 1024 * 1024


def _plan_tiles(tm_max, K, w_dtype, w_tiles, io_bytes_per_elem, tmp_bytes_per_elem, extra_bytes_per_row=0,
                tn_options=(512, 256)):
    w_bytes = jnp.dtype(w_dtype).itemsize
    w_copy_bytes = 0 if w_dtype == BF else 2
    tm = tm_max
    while tm >= 256:
        for tn in tn_options:
            need = (2 * tm * K * 2 + w_tiles * K * tn * (2 * w_bytes + w_copy_bytes)
                    + tm * tn * (2 * io_bytes_per_elem + tmp_bytes_per_elem) + tm * extra_bytes_per_row)
            if need <= MATMUL_VMEM_BUDGET:
                return tm, tn
        tm //= 2
    raise ValueError("no matmul tiling fits VMEM")


def _lhs_spec(tm, K):
    return pl.BlockSpec((None, tm, K), lambda g, m, n: (g, m, 0))


def _dot(a, w):
    return jnp.dot(a, w.astype(BF), preferred_element_type=F32)


EPILOGUE_ROWS = 256


def _row_chunks(tm):
    rows = min(EPILOGUE_ROWS, tm)
    return [slice(c * rows, (c + 1) * rows) for c in range(tm // rows)]


def _ada_kernel(c_ref, w_ref, b_ref, o_ref):
    s = _silu(c_ref[...]).astype(BF)
    o_ref[...] = _dot(s, w_ref[...]) + b_ref[...]


def ada_mod(c_rows, w_ada, b_ada, tn=1024):
    depth, d, n = w_ada.shape
    rows = c_rows.shape[0]
    return pl.pallas_call(
        _ada_kernel,
        out_shape=jax.ShapeDtypeStruct((depth, rows, n), F32),
        grid=(depth, n // tn),
        in_specs=[pl.BlockSpec((rows, d), lambda l, j: (0, 0)),
                  pl.BlockSpec((None, d, tn), lambda l, j: (l, 0, j)),
                  pl.BlockSpec((None, 1, tn), lambda l, j: (l, 0, j))],
        out_specs=pl.BlockSpec((None, rows, tn), lambda l, j: (l, 0, j)),
        compiler_params=_cparams("arbitrary", "arbitrary"),
        name="ada_mod",
    )(c_rows, w_ada, b_ada.reshape(depth, 1, n))


def _norm_mod_kernel(x_ref, g_ref, sh_ref, sc_ref, o_ref):
    x = x_ref[...]
    ms = jnp.mean(x * x, axis=-1, keepdims=True)
    y = x_ref[...] * lax.rsqrt(ms + EPS) * g_ref[...]
    o_ref[...] = (y * (1.0 + sc_ref[...]) + sh_ref[...]).astype(o_ref.dtype)


def norm_mod(x, g, mod, k_shift, k_scale, ts=512):
    G, S, D = x.shape
    ts = min(ts, S)
    return pl.pallas_call(
        _norm_mod_kernel,
        out_shape=jax.ShapeDtypeStruct((G, S, D), BF),
        grid=(G, S // ts),
        in_specs=[pl.BlockSpec((None, ts, D), lambda g_, i: (g_, i, 0)),
                  pl.BlockSpec((1, D), lambda g_, i: (0, 0)),
                  pl.BlockSpec((None, None, 1, D), lambda g_, i: (g_, k_shift, 0, 0)),
                  pl.BlockSpec((None, None, 1, D), lambda g_, i: (g_, k_scale, 0, 0))],
        out_specs=pl.BlockSpec((None, ts, D), lambda g_, i: (g_, i, 0)),
        compiler_params=_cparams("arbitrary", "arbitrary"),
        name="norm_mod",
    )(x, g.reshape(1, D), mod, mod)


def _final_norm_kernel(x_ref, g_ref, o_ref):
    x = x_ref[...]
    ms = jnp.mean(x * x, axis=-1, keepdims=True)
    o_ref[...] = x * lax.rsqrt(ms + EPS) * g_ref[...]


def final_norm(x, g, ts=512):
    G, S, D = x.shape
    ts = min(ts, S)
    return pl.pallas_call(
        _final_norm_kernel,
        out_shape=jax.ShapeDtypeStruct((G, S, D), F32),
        grid=(G, S // ts),
        in_specs=[pl.BlockSpec((None, ts, D), lambda g_, i: (g_, i, 0)),
                  pl.BlockSpec((1, D), lambda g_, i: (0, 0))],
        out_specs=pl.BlockSpec((None, ts, D), lambda g_, i: (g_, i, 0)),
        compiler_params=_cparams("arbitrary", "arbitrary"),
        name="final_norm",
    )(x, g.reshape(1, D))


def _swiglu_kernel(h_ref, wa_ref, wb_ref, o_ref):
    wa = wa_ref[...].astype(BF)
    wb = wb_ref[...].astype(BF)
    for r in _row_chunks(h_ref.shape[0]):
        h = h_ref[r, :]
        a = _dot(h, wa)
        b = _dot(h, wb)
        o_ref[r, :] = (_silu(a) * b).astype(o_ref.dtype)


def _w_spec(w_idx, block, col_block):
    lead = tuple(w_idx)
    return pl.BlockSpec((None,) * len(lead) + block,
                        lambda g, m, n: lead + (0,) * (len(block) - 1) + (col_block(n),))


def swiglu_in(h, w, w_idx, tm):
    G, S, K = h.shape
    F = w.shape[-1] // 2
    tm, tn = _plan_tiles(tm, K, w.dtype, 2, 2, 0, tn_options=(256,))
    nb = F // tn
    return pl.pallas_call(
        _swiglu_kernel,
        out_shape=jax.ShapeDtypeStruct((G, S, F), BF),
        grid=(G, S // tm, nb),
        in_specs=[_lhs_spec(tm, K),
                  _w_spec(w_idx, (K, tn), lambda n: n),
                  _w_spec(w_idx, (K, tn), lambda n: n + nb)],
        out_specs=pl.BlockSpec((None, tm, tn), lambda g, m, n: (g, m, n)),
        compiler_params=_cparams("arbitrary", "arbitrary", "arbitrary"),
        name="swiglu_in",
    )(h, w, w)


def _resid_kernel(coef, a_ref, w_ref, x_ref, g_ref, o_ref):
    y = _dot(a_ref[...], w_ref[...])
    o_ref[...] = x_ref[...] + (coef * g_ref[...]) * y


def resid_out(a, w, w_idx, x, mod, k_gate, coef, tm):
    G, S, K = a.shape
    D = x.shape[-1]
    tm, tn = _plan_tiles(tm, K, w.dtype, 1, 8, 16)
    return pl.pallas_call(
        functools.partial(_resid_kernel, coef),
        out_shape=jax.ShapeDtypeStruct((G, S, D), F32),
        grid=(G, S // tm, D // tn),
        in_specs=[_lhs_spec(tm, K),
                  _w_spec(w_idx, (K, tn), lambda n: n),
                  pl.BlockSpec((None, tm, tn), lambda g, m, n: (g, m, n)),
                  pl.BlockSpec((None, None, 1, tn), lambda g, m, n: (g, k_gate, 0, n))],
        out_specs=pl.BlockSpec((None, tm, tn), lambda g, m, n: (g, m, n)),
        compiler_params=_cparams("arbitrary", "arbitrary", "arbitrary"),
        name="resid_out",
    )(a, w, x, mod)


def _proj_kernel(act, h_ref, w_ref, *rest):
    o_ref = rest[-1]
    w = w_ref[...].astype(BF)
    for r in _row_chunks(h_ref.shape[0]):
        acc = _dot(h_ref[r, :], w)
        if act == "gelu":
            acc = jax.nn.gelu(acc)
        elif act == "sigmoid_bias":
            acc = jax.nn.sigmoid(acc + rest[0][...])
        o_ref[r, :] = acc.astype(o_ref.dtype)


def proj(h, w_in, l, col0, ncols, out_dtype, tm, act="none", bias=None):
    G, S, K = h.shape
    tm, tn = _plan_tiles(tm, K, w_in.dtype, 1, jnp.dtype(out_dtype).itemsize, 8)
    cb = col0 // tn
    in_specs = [_lhs_spec(tm, K),
                pl.BlockSpec((None, K, tn), lambda g, m, n: (l, 0, cb + n))]
    args = [h, w_in]
    if bias is not None:
        in_specs.append(pl.BlockSpec((1, tn), lambda g, m, n: (0, n)))
        args.append(bias)
    return pl.pallas_call(
        functools.partial(_proj_kernel, act),
        out_shape=jax.ShapeDtypeStruct((G, S, ncols), out_dtype),
        grid=(G, S // tm, ncols // tn),
        in_specs=in_specs,
        out_specs=pl.BlockSpec((None, tm, tn), lambda g, m, n: (g, m, n)),
        compiler_params=_cparams("arbitrary", "arbitrary", "arbitrary"),
        name="proj_" + act,
    )(*args)


def _qk_kernel(h_ref, w_ref, gain_ref, cos_ref, sin_ref, swap_ref, o_ref):
    acc = _dot(h_ref[...], w_ref[...])
    cos = cos_ref[...]
    sin = sin_ref[...]
    swap = swap_ref[...]
    for hd in range(acc.shape[1] // HEAD_DIM):
        cols = slice(hd * HEAD_DIM, (hd + 1) * HEAD_DIM)
        v = acc[:, cols]
        ms = jnp.mean(v * v, axis=-1, keepdims=True)
        y = v * lax.rsqrt(ms + EPS) * gain_ref[:, cols]
        hi = y.astype(BF)
        lo = (y - hi.astype(F32)).astype(BF)
        partner = (jnp.dot(hi, swap, preferred_element_type=F32)
                   + jnp.dot(lo, swap, preferred_element_type=F32))
        o_ref[:, cols] = (y * cos + partner * sin).astype(o_ref.dtype)


def proj_qk(h, w_in, l, ncols, gain, cos, sin, rows_per_seq, tm):
    G, S, K = h.shape
    tm, tn = _plan_tiles(tm, K, w_in.dtype, 1, 2, 8, extra_bytes_per_row=2 * 2 * HEAD_DIM * 4)
    seq_blocks = rows_per_seq // tm
    lane = jnp.arange(HEAD_DIM)
    quarter = HEAD_DIM // 4
    partner_lane = jnp.where((lane % (2 * quarter)) < quarter, lane + quarter, lane - quarter)
    swap = (lane[:, None] == partner_lane[None, :]).astype(BF)
    return pl.pallas_call(
        _qk_kernel,
        out_shape=jax.ShapeDtypeStruct((G, S, ncols), BF),
        grid=(G, S // tm, ncols // tn),
        in_specs=[_lhs_spec(tm, K),
                  pl.BlockSpec((None, K, tn), lambda g, m, n: (l, 0, n)),
                  pl.BlockSpec((1, tn), lambda g, m, n: (0, n)),
                  pl.BlockSpec((tm, HEAD_DIM), lambda g, m, n: (m % seq_blocks, 0)),
                  pl.BlockSpec((tm, HEAD_DIM), lambda g, m, n: (m % seq_blocks, 0)),
                  pl.BlockSpec((HEAD_DIM, HEAD_DIM), lambda g, m, n: (0, 0))],
        out_specs=pl.BlockSpec((None, tm, tn), lambda g, m, n: (g, m, n)),
        compiler_params=_cparams("arbitrary", "arbitrary", "arbitrary"),
        name="proj_qk",
    )(h, w_in, gain, cos, sin, swap)


def _key_split(n_keys):
    if n_keys < 3 * LANES:
        return 0
    blocks = (n_keys // 2) // LANES
    if blocks % 2 == 0:
        blocks -= 1
    return blocks * LANES


def _attn_kernel(with_latent_keys, n_cast, q_ref, kc_ref, vc_ref, *rest):
    n_in = 2 if with_latent_keys else 0
    cast_in = rest[n_in:n_in + n_cast]
    o_ref = rest[n_in + n_cast]
    cast_out = rest[n_in + n_cast + 1:]
    for w_src, w_dst in zip(cast_in, cast_out):
        w_dst[...] = w_src[...].astype(w_dst.dtype)
    if with_latent_keys:
        k = jnp.concatenate([kc_ref[...], rest[0][...]], axis=0)
        v = jnp.concatenate([vc_ref[...], rest[1][...]], axis=0)
    else:
        k, v = kc_ref[...], vc_ref[...]
    v_ones = jnp.concatenate([v, jnp.ones_like(v)], axis=1)
    split = _key_split(k.shape[0])
    for g in range(Q_GROUP):
        cols = slice(g * HEAD_DIM, (g + 1) * HEAD_DIM)
        s = lax.dot_general(q_ref[:, cols], k, (((1,), (1,)), ((), ())), preferred_element_type=F32)
        m = jnp.max(s, axis=-1, keepdims=True)
        if split:
            p_lo = jnp.exp2(s[:, :split] - m).astype(BF)
            p_hi = jnp.exp2(s[:, split:] - m).astype(BF)
            acc = (jnp.dot(p_lo, v_ones[:split], preferred_element_type=F32)
                   + jnp.dot(p_hi, v_ones[split:], preferred_element_type=F32))
        else:
            acc = jnp.dot(jnp.exp2(s - m).astype(BF), v_ones, preferred_element_type=F32)
        o_ref[:, cols] = (acc[:, :HEAD_DIM] / acc[:, HEAD_DIM:]).astype(o_ref.dtype)


def attention(qk, v, qk_ctx, v_ctx, attn_width, with_latent_keys, cast=(), tq=512):
    B, S, _ = qk.shape
    Tc = qk_ctx.shape[1]
    n_kv = v.shape[-1] // HEAD_DIM
    gw = Q_GROUP * HEAD_DIM
    kcol = attn_width // HEAD_DIM
    tq = min(tq, S)
    n_q = S // tq
    n_steps = B * n_kv * n_q
    in_specs = [pl.BlockSpec((None, tq, gw), lambda b, h, i: (b, i, h)),
                pl.BlockSpec((None, Tc, HEAD_DIM), lambda b, h, i: (b, 0, kcol + h)),
                pl.BlockSpec((None, Tc, HEAD_DIM), lambda b, h, i: (b, 0, h))]
    args = [qk, qk_ctx, v_ctx]
    if with_latent_keys:
        in_specs += [pl.BlockSpec((None, S, HEAD_DIM), lambda b, h, i: (b, 0, kcol + h)),
                     pl.BlockSpec((None, S, HEAD_DIM), lambda b, h, i: (b, 0, h))]
        args += [qk, v]
    out_shape = [jax.ShapeDtypeStruct((B, S, attn_width), BF)]
    out_specs = [pl.BlockSpec((None, tq, gw), lambda b, h, i: (b, i, h))]

    def step(b, h, i):
        return (b * n_kv + h) * n_q + i

    for w, w_idx in cast:
        rows, cols = w.shape[-2:]
        slab = rows // n_steps
        assert slab * n_steps == rows and slab % BF16_SUBLANES == 0, (rows, n_steps)
        lead = tuple(w_idx)
        in_specs.append(pl.BlockSpec((None,) * len(lead) + (slab, cols),
                                     lambda b, h, i, lead=lead: lead + (step(b, h, i), 0)))
        args.append(w)
        out_shape.append(jax.ShapeDtypeStruct((rows, cols), BF))
        out_specs.append(pl.BlockSpec((slab, cols), lambda b, h, i: (step(b, h, i), 0)))
    outs = pl.pallas_call(
        functools.partial(_attn_kernel, with_latent_keys, len(cast)),
        out_shape=tuple(out_shape),
        grid=(B, n_kv, n_q),
        in_specs=in_specs,
        out_specs=tuple(out_specs),
        compiler_params=pltpu.CompilerParams(dimension_semantics=("arbitrary",) * 3,
                                             vmem_limit_bytes=ATTENTION_VMEM_LIMIT_BYTES),
        name="attention",
    )(*args)
    return outs[0], list(outs[1:])


def _one_minus_sq(a, y):
    poly = y * (1.0 / 120.0) + (1.0 / 24.0)
    for coef in (1.0 / 6.0, 0.5, 1.0):
        poly = poly * y + coef
    return jnp.where(y > -0.0625, -(poly * y), 1.0 - a * a)


def _tile_scan(a, b, reverse):
    row = lax.broadcasted_iota(jnp.int32, a.shape, 0)
    for d in (1, 2, 4):
        shift = SUBLANES - d if reverse else d
        valid = (row < SUBLANES - d) if reverse else (row >= d)
        a_prev = jnp.where(valid, pltpu.roll(a, shift, 0), 1.0)
        b_prev = jnp.where(valid, pltpu.roll(b, shift, 0), 0.0)
        b = a * b_prev + b
        a = a * a_prev
    return a, b


def _lru_kernel(S, tchunk, unroll, x_ref, gy_ref, h0_ref, cw_ref, cb_ref, wa_ref, ba_ref, wx_ref, bx_ref,
                lam_ref, o_ref, hfin_ref, xs_ref, a_ref, b_ref, h_ref):
    zeros = jnp.zeros((CONV_PAD, LRU_BLOCK), F32)
    xs_ref[0:CONV_PAD, :] = zeros
    xs_ref[CONV_PAD + S:2 * CONV_PAD + S, :] = zeros
    xs_ref[CONV_PAD:CONV_PAD + S, :] = x_ref[...]

    neg_lam = -lam_ref[...]
    softplus = jnp.maximum(neg_lam, 0.0) + jnp.log1p(jnp.exp(-jnp.abs(neg_lam)))
    log_a_per_r = -LRU_C * softplus
    log2_a_per_r = log_a_per_r * math.log2(math.e)

    for c in range(S // tchunk):
        t0 = c * tchunk
        xc = cb_ref[...]
        for j in range(CONV_W):
            xc = xc + xs_ref[pl.ds(CONV_PAD + t0 + j - CONV_LEFT, tchunk), :] * cw_ref[j:j + 1, :]
        xb = xc.astype(BF)
        for d in range(2):
            r = jax.nn.sigmoid(jnp.dot(xb, wa_ref[d], preferred_element_type=F32) + ba_ref[d:d + 1, :])
            i = jax.nn.sigmoid(jnp.dot(xb, wx_ref[d], preferred_element_type=F32) + bx_ref[d:d + 1, :])
            a = jnp.exp2(r * log2_a_per_r[d:d + 1, :])
            a_ref[d, pl.ds(t0, tchunk), :] = a
            one_minus_a2 = _one_minus_sq(a, r * (2.0 * log_a_per_r[d:d + 1, :]))
            b_ref[d, pl.ds(t0, tchunk), :] = jnp.sqrt(one_minus_a2) * (i * xc)

    n_tiles = S // SUBLANES

    def body(i, carry):
        hf, hr = carry
        tf = pl.multiple_of(i * SUBLANES, SUBLANES)
        af, bf = _tile_scan(a_ref[0, pl.ds(tf, SUBLANES), :], b_ref[0, pl.ds(tf, SUBLANES), :], False)
        h = bf + af * hf
        h_ref[0, pl.ds(tf, SUBLANES), :] = h
        hf = h[SUBLANES - 1:SUBLANES, :]
        tr = pl.multiple_of((n_tiles - 1 - i) * SUBLANES, SUBLANES)
        ar, br = _tile_scan(a_ref[1, pl.ds(tr, SUBLANES), :], b_ref[1, pl.ds(tr, SUBLANES), :], True)
        h = br + ar * hr
        h_ref[1, pl.ds(tr, SUBLANES), :] = h
        hr = h[0:1, :]
        return hf, hr

    hf, hr = lax.fori_loop(0, n_tiles, body, (h0_ref[0:1, :], h0_ref[1:2, :]), unroll=unroll)
    hfin_ref[0:1, :] = hf
    hfin_ref[1:2, :] = hr
    o_ref[...] = ((h_ref[0] + h_ref[1]) * gy_ref[...]).astype(o_ref.dtype)


def lru_mix(xseg, gy, h0, conv_w, conv_b, wa, ba, wx, bx, lam, l, tchunk=512, unroll=4):
    B, S, W = xseg.shape
    tchunk = min(tchunk, S)
    nblk = W // LRU_BLOCK
    seq = pl.BlockSpec((None, S, LRU_BLOCK), lambda b, n: (b, 0, n))
    st = pl.BlockSpec((None, 2, LRU_BLOCK), lambda b, n: (b, 0, n))
    vec2 = pl.BlockSpec((None, 2, LRU_BLOCK), lambda b, n: (l, 0, n))
    wblk = pl.BlockSpec((None, 2, None, LRU_BLOCK, LRU_BLOCK), lambda b, n: (l, 0, n, 0, 0))
    return pl.pallas_call(
        functools.partial(_lru_kernel, S, tchunk, unroll),
        out_shape=(jax.ShapeDtypeStruct((B, S, W), BF), jax.ShapeDtypeStruct((B, 2, W), F32)),
        grid=(B, nblk),
        in_specs=[seq, seq, st,
                  pl.BlockSpec((None, CONV_W, LRU_BLOCK), lambda b, n: (l, 0, n)),
                  pl.BlockSpec((None, 1, LRU_BLOCK), lambda b, n: (l, 0, n)),
                  wblk, vec2, wblk, vec2, vec2],
        out_specs=(seq, st),
        scratch_shapes=[pltpu.VMEM((S + 2 * CONV_PAD, LRU_BLOCK), F32),
                        pltpu.VMEM((2, S, LRU_BLOCK), F32),
                        pltpu.VMEM((2, S, LRU_BLOCK), F32),
                        pltpu.VMEM((2, S, LRU_BLOCK), F32)],
        compiler_params=_cparams("arbitrary", "arbitrary"),
        name="lru_mix",
    )(xseg, gy, h0, conv_w, conv_b.reshape(conv_b.shape[0], 1, W), wa, ba, wx, bx, lam)


def _dft_tables(n):
    k = jnp.arange(n, dtype=jnp.int32)
    ang = ((k[:, None] * k[None, :]) % n).astype(F32) * (2.0 * math.pi / n)
    return jnp.cos(ang), jnp.sin(ang)


def _chan_dft_kernel(f_ref, cs_ref, zc_ref, zs_ref):
    w = zc_ref.shape[-1]
    z = jnp.dot(f_ref[...], cs_ref[...], preferred_element_type=F32)
    zc_ref[...] = z[:, :w].astype(zc_ref.dtype)
    zs_ref[...] = z[:, w:].astype(zs_ref.dtype)


def chan_dft(f, cs, tm, out_dtype):
    G, S, W = f.shape
    gw = W // FNET_GROUPS
    blk = pl.BlockSpec((None, tm, gw), lambda g, m, n: (g, m, n))
    return pl.pallas_call(
        _chan_dft_kernel,
        out_shape=(jax.ShapeDtypeStruct((G, S, W), out_dtype),) * 2,
        grid=(G, S // tm, FNET_GROUPS),
        in_specs=[blk, pl.BlockSpec((gw, 2 * gw), lambda g, m, n: (0, 0))],
        out_specs=(blk, blk),
        compiler_params=_cparams("arbitrary", "arbitrary", "arbitrary"),
        name="chan_dft",
    )(f, cs)


def _time_dft_kernel(scale, c_ref, s_ref, zc_ref, zs_ref, o_ref):
    re = jnp.dot(c_ref[...], zc_ref[...], preferred_element_type=F32)
    re = re - jnp.dot(s_ref[...], zs_ref[...], preferred_element_type=F32)
    o_ref[...] = (re * scale).astype(o_ref.dtype)


def time_dft_dense(zc, zs, scale, tm=512, tn=512):
    B, T, W = zc.shape
    tm = min(tm, T)
    ct, st = (t.astype(BF) for t in _dft_tables(T))
    a_spec = pl.BlockSpec((tm, T), lambda b, n, m: (m, 0))
    z_spec = pl.BlockSpec((None, T, tn), lambda b, n, m: (b, 0, n))
    return pl.pallas_call(
        functools.partial(_time_dft_kernel, scale),
        out_shape=jax.ShapeDtypeStruct((B, T, W), BF),
        grid=(B, W // tn, T // tm),
        in_specs=[a_spec, a_spec, z_spec, z_spec],
        out_specs=pl.BlockSpec((None, tm, tn), lambda b, n, m: (b, m, n)),
        compiler_params=_cparams("arbitrary", "arbitrary", "arbitrary"),
        name="time_dft",
    )(ct, st, zc, zs)


def _dft_stage1_kernel(zc_ref, zs_ref, rot_ref, twc_ref, tws_ref, oc_ref, os_ref):
    n1 = zc_ref.shape[0]
    tn = zc_ref.shape[-1]
    rot = rot_ref[...]
    for j in range(zc_ref.shape[1]):
        z = jnp.concatenate([zc_ref[:, j, :], zs_ref[:, j, :]], axis=0).astype(BF)
        res = jnp.dot(rot, z, preferred_element_type=F32)
        ac, as_ = res[:n1], res[n1:]
        twc = jnp.tile(twc_ref[j], (1, tn // LANES))
        tws = jnp.tile(tws_ref[j], (1, tn // LANES))
        oc_ref[j] = (ac * twc - as_ * tws).astype(oc_ref.dtype)
        os_ref[j] = (as_ * twc + ac * tws).astype(os_ref.dtype)


def dft_stage1(zc, zs, n1, n2, tn=512):
    B, T, W = zc.shape
    c, s = _dft_tables(n1)
    rot = jnp.concatenate([jnp.concatenate([c, -s], axis=1), jnp.concatenate([s, c], axis=1)], axis=0).astype(BF)
    cc = jnp.arange(n1, dtype=jnp.int32)
    bb = jnp.arange(n2, dtype=jnp.int32)
    ang = (bb[:, None] * cc[None, :]).astype(F32) * (2.0 * math.pi / T)
    twc = jnp.broadcast_to(jnp.cos(ang)[:, :, None], (n2, n1, LANES))
    tws = jnp.broadcast_to(jnp.sin(ang)[:, :, None], (n2, n1, LANES))
    z_spec = pl.BlockSpec((None, n1, SUBLANES, tn), lambda b, i, n: (b, 0, i, n))
    tw_spec = pl.BlockSpec((SUBLANES, n1, LANES), lambda b, i, n: (i, 0, 0))
    o_spec = pl.BlockSpec((None, SUBLANES, n1, tn), lambda b, i, n: (b, i, 0, n))
    return pl.pallas_call(
        _dft_stage1_kernel,
        out_shape=(jax.ShapeDtypeStruct((B, n2, n1, W), BF),) * 2,
        grid=(B, n2 // SUBLANES, W // tn),
        in_specs=[z_spec, z_spec, pl.BlockSpec((2 * n1, 2 * n1), lambda b, i, n: (0, 0)), tw_spec, tw_spec],
        out_specs=(o_spec, o_spec),
        compiler_params=_cparams("arbitrary", "arbitrary", "arbitrary"),
        name="dft_stage1",
    )(zc.reshape(B, n1, n2, W), zs.reshape(B, n1, n2, W), rot, twc, tws)


def _dft_stage2_kernel(scale, ac_ref, as_ref, cs_ref, o_ref, ac_f32, as_f32, o_f32):
    cs = cs_ref[...]
    ac_f32[...] = ac_ref[...].astype(F32)
    as_f32[...] = as_ref[...].astype(F32)
    for j in range(ac_ref.shape[1]):
        a = jnp.concatenate([ac_f32[:, j, :], as_f32[:, j, :]], axis=0).astype(BF)
        o_f32[:, j, :] = jnp.dot(cs, a, preferred_element_type=F32) * scale
    o_ref[...] = o_f32[...].astype(o_ref.dtype)


def dft_stage2(ac, as_, scale, tn=512):
    B, n2, n1, W = ac.shape
    c, s = _dft_tables(n2)
    cs = jnp.concatenate([c, -s], axis=1).astype(BF)
    cblk = BF16_SUBLANES
    a_spec = pl.BlockSpec((None, n2, cblk, tn), lambda b, i, n: (b, 0, i, n))
    out = pl.pallas_call(
        functools.partial(_dft_stage2_kernel, scale),
        out_shape=jax.ShapeDtypeStruct((B, n2, n1, W), BF),
        grid=(B, n1 // cblk, W // tn),
        in_specs=[a_spec, a_spec, pl.BlockSpec((n2, 2 * n2), lambda b, i, n: (0, 0))],
        out_specs=a_spec,
        scratch_shapes=[pltpu.VMEM((n2, cblk, tn), F32)] * 3,
        compiler_params=_cparams("arbitrary", "arbitrary", "arbitrary"),
        name="dft_stage2",
    )(ac, as_, cs)
    return out.reshape(B, n2 * n1, W)


def _two_stage_dft_ok(T):
    n = math.isqrt(T)
    return n * n == T and n % BF16_SUBLANES == 0


def fourier_branch(f, cs_chan, tm, B, T, scale):
    W = f.shape[-1]
    if _two_stage_dft_ok(T):
        n = math.isqrt(T)
        zc, zs = chan_dft(f, cs_chan, tm, F32)
        ac, as_ = dft_stage1(zc.reshape(B, T, W), zs.reshape(B, T, W), n, n)
        return dft_stage2(ac, as_, scale)
    zc, zs = chan_dft(f, cs_chan, tm, BF)
    return time_dft_dense(zc.reshape(B, T, W), zs.reshape(B, T, W), scale)


def _merge_kernel(b0_ref, b1_ref, b2_ref, w_ref, g0_ref, g1_ref, g2_ref, o_ref):
    acc = g0_ref[...] * _dot(b0_ref[...], w_ref[0])
    acc = acc + g1_ref[...] * _dot(b1_ref[...], w_ref[1])
    acc = acc + g2_ref[...] * _dot(b2_ref[...], w_ref[2])
    o_ref[...] = acc.astype(o_ref.dtype)


def gated_merge(branches, w_branch, w_idx, gates, tm):
    G, S, K = branches[0].shape
    tm = min(tm, 1024)
    tn = 256
    D = w_branch.shape[-1]
    nb = D // tn
    b_spec = pl.BlockSpec((None, tm, K), lambda g, m, n: (g, m, 0))

    def g_spec(j):
        return pl.BlockSpec((None, tm, tn), lambda g, m, n: (g, m, j * nb + n))

    return pl.pallas_call(
        _merge_kernel,
        out_shape=jax.ShapeDtypeStruct((G, S, D), BF),
        grid=(G, S // tm, nb),
        in_specs=[b_spec, b_spec, b_spec,
                  _w_spec(w_idx, (N_BRANCH, K, tn), lambda n: n),
                  g_spec(0), g_spec(1), g_spec(2)],
        out_specs=pl.BlockSpec((None, tm, tn), lambda g, m, n: (g, m, n)),
        compiler_params=_cparams("arbitrary", "arbitrary", "arbitrary"),
        name="gated_merge",
    )(*branches, w_branch, gates, gates, gates)


def _rope_tables(n_tokens):
    rows = n_tokens // GRID_W
    row = jnp.repeat(jnp.arange(rows, dtype=F32), GRID_W)
    col = jnp.tile(jnp.arange(GRID_W, dtype=F32), rows)
    axis_dim = HEAD_DIM // 2
    inv_freq = ROPE_THETA ** (-jnp.arange(0, axis_dim, 2, dtype=F32) / axis_dim)
    ar, ac = row[:, None] * inv_freq, col[:, None] * inv_freq
    cos = jnp.concatenate([jnp.cos(ar), jnp.cos(ar), jnp.cos(ac), jnp.cos(ac)], axis=-1)
    sin = jnp.concatenate([-jnp.sin(ar), jnp.sin(ar), -jnp.sin(ac), jnp.sin(ac)], axis=-1)
    return cos, sin


def kernel(x, c, ctx, c_ctx, w_ada, b_ada, g_norm, w_ff_in, w_ff_out, w_in, b_gate, q_gain, k_gain,
           conv_w, conv_b, lru_wa, lru_ba, lru_wx, lru_bx, lru_lam, w_branch, w_out, g_final):
    B, S, D = x.shape
    Tc = ctx.shape[1]
    depth = w_ada.shape[0]
    lru_w = conv_w.shape[-1]
    fnet_w = D // 2
    attn_w = (D // 256) * HEAD_DIM
    kv_w = attn_w // Q_GROUP
    v0 = attn_w + kv_w
    x0 = v0 + kv_w
    y0 = x0 + lru_w
    f0 = y0 + lru_w
    g0 = f0 + fnet_w

    lru_wa_b = lru_wa.astype(BF)
    lru_wx_b = lru_wx.astype(BF)

    n_rows = SUBLANES
    c_rows = jnp.concatenate([c, c_ctx[None, :], jnp.zeros((n_rows - B - 1, D), F32)], axis=0)
    mod = ada_mod(c_rows, w_ada, b_ada).reshape(depth, n_rows, N_MOD, 1, D)
    mod_lat = mod[:, :B]
    mod_ctx = mod[:, B:B + 1]

    cos_l, sin_l = _rope_tables(S)
    cos_c, sin_c = jnp.ones((B * Tc, HEAD_DIM), F32), jnp.zeros((B * Tc, HEAD_DIM), F32)
    q_fold = HEAD_DIM ** -0.5 * math.log2(math.e)
    gain_qk = jnp.concatenate([jnp.tile(q_gain * q_fold, (1, attn_w // HEAD_DIM)),
                               jnp.tile(k_gain, (1, kv_w // HEAD_DIM))], axis=1)
    gw = fnet_w // FNET_GROUPS
    cg, sg = _dft_tables(gw)
    cs_chan = jnp.concatenate([cg, sg], axis=1).astype(BF)

    tm_ffn = min(2048, S)
    tm_l = min(1024, S)
    tm_c = B * Tc
    xl = x
    xc = ctx.reshape(1, B * Tc, D)
    mods = {"lat": mod_lat, "ctx": mod_ctx}

    wts = {}
    for l in range(depth):
        wts["branch", l] = (w_branch, (l,))
        wts["out", l] = (w_out, (l,))
        for i in range(2):
            wts["ff_in", l, i] = (w_ff_in, (l, i))
            wts["ff_out", l, i] = (w_ff_out, (l, i))
    w_branch_rows = w_branch.reshape(depth, N_BRANCH * w_branch.shape[2], D)

    def ffn(xs, which, l, i, k_mod):
        lat = which == "lat"
        mod_l = mods[which][l]
        h = norm_mod(xs, g_norm[l, 2 * i], mod_l, k_mod, k_mod + 1)
        act = swiglu_in(h, *wts["ff_in", l, i], tm_ffn if lat else tm_c)
        return resid_out(act, *wts["ff_out", l, i], xs, mod_l, k_mod + 2, 0.5, tm_l if lat else tm_c)

    for l in range(depth):
        last = l == depth - 1
        xl = ffn(xl, "lat", l, 0, 0)
        xc = ffn(xc, "ctx", l, 0, 0)

        hl = norm_mod(xl, g_norm[l, 1], mod_lat[l], 3, 4)
        hc = norm_mod(xc, g_norm[l, 1], mod_ctx[l], 3, 4)
        gain = gain_qk[l:l + 1]
        bias_g = b_gate[l].reshape(1, N_BRANCH * D)

        qk_c = proj_qk(hc, w_in, l, attn_w + kv_w, gain, cos_c, sin_c, B * Tc, tm_c).reshape(B, Tc, -1)
        v_c = proj(hc, w_in, l, v0, kv_w, BF, tm_c).reshape(B, Tc, kv_w)
        xs_c = proj(hc, w_in, l, x0, lru_w, F32, tm_c).reshape(B, Tc, lru_w)
        if last:
            gy_c = jnp.zeros((B, Tc, lru_w), F32)
        else:
            gy_c = proj(hc, w_in, l, y0, lru_w, F32, tm_c, act="gelu").reshape(B, Tc, lru_w)
        lru_args = (conv_w, conv_b, lru_wa_b, lru_ba, lru_wx_b, lru_bx, lru_lam, l)
        lru_c, h_fin = lru_mix(xs_c, gy_c, jnp.zeros((B, 2, lru_w), F32), *lru_args)

        qk_l = proj_qk(hl, w_in, l, attn_w + kv_w, gain, cos_l, sin_l, S, tm_l)
        v_l = proj(hl, w_in, l, v0, kv_w, BF, tm_l)
        xs_l = proj(hl, w_in, l, x0, lru_w, F32, tm_l)
        gy_l = proj(hl, w_in, l, y0, lru_w, F32, tm_l, act="gelu")
        f_l = proj(hl, w_in, l, f0, fnet_w, BF, tm_l)
        gate_l = proj(hl, w_in, l, g0, N_BRANCH * D, F32, tm_l, act="sigmoid_bias", bias=bias_g)

        cast_keys = [("branch", l), ("out", l), ("ff_in", l, 1), ("ff_out", l, 1)]
        if not last:
            cast_keys += [("ff_in", l + 1, 0), ("ff_out", l + 1, 0)]
        cast = [(w_branch_rows, (l,)) if key[0] == "branch" else wts[key] for key in cast_keys]
        attn_l, copies = attention(qk_l, v_l, qk_c, v_c, attn_w, True, cast=cast)
        for key, w_bf in zip(cast_keys, copies):
            if key[0] == "branch":
                w_bf = w_bf.reshape(w_branch.shape[1:])
            wts[key] = (w_bf, ())

        lru_l, _ = lru_mix(xs_l, gy_l, h_fin, *lru_args)
        four_l = fourier_branch(f_l, cs_chan, tm_l, B, S, 1.0 / math.sqrt(S * gw))
        merged = gated_merge((attn_l, lru_l, four_l), *wts["branch", l], gate_l, tm_l)
        xl = resid_out(merged, *wts["out", l], xl, mod_lat[l], 5, 1.0, tm_l)
        xl = ffn(xl, "lat", l, 1, 6)

        if not last:
            f_c = proj(hc, w_in, l, f0, fnet_w, BF, tm_c)
            gate_c = proj(hc, w_in, l, g0, N_BRANCH * D, F32, tm_c, act="sigmoid_bias", bias=bias_g)
            attn_c, _ = attention(qk_c, v_c, qk_c, v_c, attn_w, False)
            four_c = fourier_branch(f_c, cs_chan, tm_c, B, Tc, 1.0 / math.sqrt(Tc * gw))
            merged_c = gated_merge((attn_c.reshape(1, B * Tc, attn_w), lru_c.reshape(1, B * Tc, lru_w),
                                    four_c.reshape(1, B * Tc, fnet_w)), *wts["branch", l], gate_c, tm_c)
            xc = resid_out(merged_c, *wts["out", l], xc, mod_ctx[l], 5, 1.0, tm_c)
            xc = ffn(xc, "ctx", l, 1, 6)

    return final_norm(xl, g_final)
```

```python
import functools
import math

import jax
import jax.numpy as jnp
from jax import lax
from jax.experimental import pallas as pl
from jax.experimental.pallas import tpu as pltpu

F32 = jnp.float32
BF = jnp.bfloat16

VMEM_LIMIT_BYTES = 56 * 1024 * 1024
ATTENTION_VMEM_LIMIT_BYTES = 60 * 1024 * 1024
LANES = 128
SUBLANES = 8
BF16_SUBLANES = 16

HEAD_DIM = 128
Q_GROUP = 4
GRID_W = 64
CONV_W = 4
CONV_LEFT = 2
LRU_C = 8.0
LRU_BLOCK = 128
FNET_GROUPS = 4
N_BRANCH = 3
N_MOD = 9
ROPE_THETA = 10000.0
EPS = 1e-6
CONV_PAD = 8


def _cparams(*sem):
    return pltpu.CompilerParams(dimension_semantics=sem, vmem_limit_bytes=VMEM_LIMIT_BYTES)


def _silu(a):
    return a * jax.nn.sigmoid(a)


MATMUL_VMEM_BUDGET = 55 * 1024 * 1024


def _plan_tiles(tm_max, K, w_dtype, w_tiles, io_bytes_per_elem, tmp_bytes_per_elem, extra_bytes_per_row=0,
                tn_options=(512, 256)):
    w_bytes = jnp.dtype(w_dtype).itemsize
    w_copy_bytes = 0 if w_dtype == BF else 2
    tm = tm_max
    while tm >= 256:
        for tn in tn_options:
            need = (2 * tm * K * 2 + w_tiles * K * tn * (2 * w_bytes + w_copy_bytes)
                    + tm * tn * (2 * io_bytes_per_elem + tmp_bytes_per_elem) + tm * extra_bytes_per_row)
            if need <= MATMUL_VMEM_BUDGET:
                return tm, tn
        tm //= 2
    raise ValueError("no matmul tiling fits VMEM")


def _lhs_spec(tm, K):
    return pl.BlockSpec((None, tm, K), lambda g, m, n: (g, m, 0))


def _dot(a, w):
    return jnp.dot(a, w.astype(BF), preferred_element_type=F32)


EPILOGUE_ROWS = 256


def _row_chunks(tm):
    rows = min(EPILOGUE_ROWS, tm)
    return [slice(c * rows, (c + 1) * rows) for c in range(tm // rows)]


def _ada_kernel(c_ref, w_ref, b_ref, o_ref):
    s = _silu(c_ref[...]).astype(BF)
    o_ref[...] = _dot(s, w_ref[...]) + b_ref[...]


def ada_mod(c_rows, w_ada, b_ada, tn=1024):
    depth, d, n = w_ada.shape
    rows = c_rows.shape[0]
    return pl.pallas_call(
        _ada_kernel,
        out_shape=jax.ShapeDtypeStruct((depth, rows, n), F32),
        grid=(depth, n // tn),
        in_specs=[pl.BlockSpec((rows, d), lambda l, j: (0, 0)),
                  pl.BlockSpec((None, d, tn), lambda l, j: (l, 0, j)),
                  pl.BlockSpec((None, 1, tn), lambda l, j: (l, 0, j))],
        out_specs=pl.BlockSpec((None, rows, tn), lambda l, j: (l, 0, j)),
        compiler_params=_cparams("arbitrary", "arbitrary"),
        name="ada_mod",
    )(c_rows, w_ada, b_ada.reshape(depth, 1, n))


def _norm_mod_kernel(x_ref, g_ref, sh_ref, sc_ref, o_ref):
    x = x_ref[...]
    ms = jnp.mean(x * x, axis=-1, keepdims=True)
    y = x_ref[...] * lax.rsqrt(ms + EPS) * g_ref[...]
    o_ref[...] = (y * (1.0 + sc_ref[...]) + sh_ref[...]).astype(o_ref.dtype)


def norm_mod(x, g, mod, k_shift, k_scale, ts=512):
    G, S, D = x.shape
    ts = min(ts, S)
    return pl.pallas_call(
        _norm_mod_kernel,
        out_shape=jax.ShapeDtypeStruct((G, S, D), BF),
        grid=(G, S // ts),
        in_specs=[pl.BlockSpec((None, ts, D), lambda g_, i: (g_, i, 0)),
                  pl.BlockSpec((1, D), lambda g_, i: (0, 0)),
                  pl.BlockSpec((None, None, 1, D), lambda g_, i: (g_, k_shift, 0, 0)),
                  pl.BlockSpec((None, None, 1, D), lambda g_, i: (g_, k_scale, 0, 0))],
        out_specs=pl.BlockSpec((None, ts, D), lambda g_, i: (g_, i, 0)),
        compiler_params=_cparams("arbitrary", "arbitrary"),
        name="norm_mod",
    )(x, g.reshape(1, D), mod, mod)


def _final_norm_kernel(x_ref, g_ref, o_ref):
    x = x_ref[...]
    ms = jnp.mean(x * x, axis=-1, keepdims=True)
    o_ref[...] = x * lax.rsqrt(ms + EPS) * g_ref[...]


def final_norm(x, g, ts=512):
    G, S, D = x.shape
    ts = min(ts, S)
    return pl.pallas_call(
        _final_norm_kernel,
        out_shape=jax.ShapeDtypeStruct((G, S, D), F32),
        grid=(G, S // ts),
        in_specs=[pl.BlockSpec((None, ts, D), lambda g_, i: (g_, i, 0)),
                  pl.BlockSpec((1, D), lambda g_, i: (0, 0))],
        out_specs=pl.BlockSpec((None, ts, D), lambda g_, i: (g_, i, 0)),
        compiler_params=_cparams("arbitrary", "arbitrary"),
        name="final_norm",
    )(x, g.reshape(1, D))


def _swiglu_kernel(h_ref, wa_ref, wb_ref, o_ref):
    wa = wa_ref[...].astype(BF)
    wb = wb_ref[...].astype(BF)
    for r in _row_chunks(h_ref.shape[0]):
        h = h_ref[r, :]
        a = _dot(h, wa)
        b = _dot(h, wb)
        o_ref[r, :] = (_silu(a) * b).astype(o_ref.dtype)


def _w_spec(w_idx, block, col_block):
    lead = tuple(w_idx)
    return pl.BlockSpec((None,) * len(lead) + block,
                        lambda g, m, n: lead + (0,) * (len(block) - 1) + (col_block(n),))


def swiglu_in(h, w, w_idx, tm):
    G, S, K = h.shape
    F = w.shape[-1] // 2
    tm, tn = _plan_tiles(tm, K, w.dtype, 2, 2, 0, tn_options=(256,))
    nb = F // tn
    return pl.pallas_call(
        _swiglu_kernel,
        out_shape=jax.ShapeDtypeStruct((G, S, F), BF),
        grid=(G, S // tm, nb),
        in_specs=[_lhs_spec(tm, K),
                  _w_spec(w_idx, (K, tn), lambda n: n),
                  _w_spec(w_idx, (K, tn), lambda n: n + nb)],
        out_specs=pl.BlockSpec((None, tm, tn), lambda g, m, n: (g, m, n)),
        compiler_params=_cparams("arbitrary", "arbitrary", "arbitrary"),
        name="swiglu_in",
    )(h, w, w)


def _resid_kernel(coef, a_ref, w_ref, x_ref, g_ref, o_ref):
    y = _dot(a_ref[...], w_ref[...])
    o_ref[...] = x_ref[...] + (coef * g_ref[...]) * y


def resid_out(a, w, w_idx, x, mod, k_gate, coef, tm):
    G, S, K = a.shape
    D = x.shape[-1]
    tm, tn = _plan_tiles(tm, K, w.dtype, 1, 8, 16)
    return pl.pallas_call(
        functools.partial(_resid_kernel, coef),
        out_shape=jax.ShapeDtypeStruct((G, S, D), F32),
        grid=(G, S // tm, D // tn),
        in_specs=[_lhs_spec(tm, K),
                  _w_spec(w_idx, (K, tn), lambda n: n),
                  pl.BlockSpec((None, tm, tn), lambda g, m, n: (g, m, n)),
                  pl.BlockSpec((None, None, 1, tn), lambda g, m, n: (g, k_gate, 0, n))],
        out_specs=pl.BlockSpec((None, tm, tn), lambda g, m, n: (g, m, n)),
        compiler_params=_cparams("arbitrary", "arbitrary", "arbitrary"),
        name="resid_out",
    )(a, w, x, mod)


def _proj_kernel(act, h_ref, w_ref, *rest):
    o_ref = rest[-1]
    w = w_ref[...].astype(BF)
    for r in _row_chunks(h_ref.shape[0]):
        acc = _dot(h_ref[r, :], w)
        if act == "gelu":
            acc = jax.nn.gelu(acc)
        elif act == "sigmoid_bias":
            acc = jax.nn.sigmoid(acc + rest[0][...])
        o_ref[r, :] = acc.astype(o_ref.dtype)


def proj(h, w_in, l, col0, ncols, out_dtype, tm, act="none", bias=None):
    G, S, K = h.shape
    tm, tn = _plan_tiles(tm, K, w_in.dtype, 1, jnp.dtype(out_dtype).itemsize, 8)
    cb = col0 // tn
    in_specs = [_lhs_spec(tm, K),
                pl.BlockSpec((None, K, tn), lambda g, m, n: (l, 0, cb + n))]
    args = [h, w_in]
    if bias is not None:
        in_specs.append(pl.BlockSpec((1, tn), lambda g, m, n: (0, n)))
        args.append(bias)
    return pl.pallas_call(
        functools.partial(_proj_kernel, act),
        out_shape=jax.ShapeDtypeStruct((G, S, ncols), out_dtype),
        grid=(G, S // tm, ncols // tn),
        in_specs=in_specs,
        out_specs=pl.BlockSpec((None, tm, tn), lambda g, m, n: (g, m, n)),
        compiler_params=_cparams("arbitrary", "arbitrary", "arbitrary"),
        name="proj_" + act,
    )(*args)


def _proj_segments_kernel(segments, h_ref, w_ref, *o_refs):
    n = pl.program_id(2)
    w = w_ref[...].astype(BF)
    for (lo, hi, act), o_ref in zip(segments, o_refs):
        @pl.when((n >= lo) & (n < hi))
        def _(act=act, o_ref=o_ref):
            for r in _row_chunks(h_ref.shape[0]):
                acc = _dot(h_ref[r, :], w)
                if act == "gelu":
                    acc = jax.nn.gelu(acc)
                o_ref[r, :] = acc.astype(o_ref.dtype)


def proj_segments(h, w_in, l, col0, segments, tm, tn=512):
    G, S, K = h.shape
    bounds, lo = [], 0
    for ncols, _, act in segments:
        bounds.append((lo, lo + ncols // tn, act))
        lo += ncols // tn
    cb = col0 // tn

    def out_spec(lo, hi):
        return pl.BlockSpec((None, tm, tn), lambda g, m, n: (g, m, jnp.clip(n - lo, 0, hi - lo - 1)))

    return pl.pallas_call(
        functools.partial(_proj_segments_kernel, tuple(bounds)),
        out_shape=tuple(jax.ShapeDtypeStruct((G, S, ncols), dt) for ncols, dt, _ in segments),
        grid=(G, S // tm, lo),
        in_specs=[_lhs_spec(tm, K),
                  pl.BlockSpec((None, K, tn), lambda g, m, n: (l, 0, cb + n))],
        out_specs=tuple(out_spec(a, b) for a, b, _ in bounds),
        compiler_params=_cparams("arbitrary", "arbitrary", "arbitrary"),
        name="proj_segments",
    )(h, w_in)


def _qk_kernel(h_ref, w_ref, gain_ref, cos_ref, sin_ref, swap_ref, o_ref):
    acc = _dot(h_ref[...], w_ref[...])
    cos = cos_ref[...]
    sin = sin_ref[...]
    swap = swap_ref[...]
    for hd in range(acc.shape[1] // HEAD_DIM):
        cols = slice(hd * HEAD_DIM, (hd + 1) * HEAD_DIM)
        v = acc[:, cols]
        ms = jnp.mean(v * v, axis=-1, keepdims=True)
        y = v * lax.rsqrt(ms + EPS) * gain_ref[:, cols]
        hi = y.astype(BF)
        lo = (y - hi.astype(F32)).astype(BF)
        partner = (jnp.dot(hi, swap, preferred_element_type=F32)
                   + jnp.dot(lo, swap, preferred_element_type=F32))
        o_ref[:, cols] = (y * cos + partner * sin).astype(o_ref.dtype)


def proj_qk(h, w_in, l, ncols, gain, cos, sin, rows_per_seq, tm):
    G, S, K = h.shape
    tm, tn = _plan_tiles(tm, K, w_in.dtype, 1, 2, 8, extra_bytes_per_row=2 * 2 * HEAD_DIM * 4)
    seq_blocks = rows_per_seq // tm
    lane = jnp.arange(HEAD_DIM)
    quarter = HEAD_DIM // 4
    partner_lane = jnp.where((lane % (2 * quarter)) < quarter, lane + quarter, lane - quarter)
    swap = (lane[:, None] == partner_lane[None, :]).astype(BF)
    return pl.pallas_call(
        _qk_kernel,
        out_shape=jax.ShapeDtypeStruct((G, S, ncols), BF),
        grid=(G, S // tm, ncols // tn),
        in_specs=[_lhs_spec(tm, K),
                  pl.BlockSpec((None, K, tn), lambda g, m, n: (l, 0, n)),
                  pl.BlockSpec((1, tn), lambda g, m, n: (0, n)),
                  pl.BlockSpec((tm, HEAD_DIM), lambda g, m, n: (m % seq_blocks, 0)),
                  pl.BlockSpec((tm, HEAD_DIM), lambda g, m, n: (m % seq_blocks, 0)),
                  pl.BlockSpec((HEAD_DIM, HEAD_DIM), lambda g, m, n: (0, 0))],
        out_specs=pl.BlockSpec((None, tm, tn), lambda g, m, n: (g, m, n)),
        compiler_params=_cparams("arbitrary", "arbitrary", "arbitrary"),
        name="proj_qk",
    )(h, w_in, gain, cos, sin, swap)


def _key_split(n_keys):
    if n_keys < 3 * LANES:
        return 0
    blocks = (n_keys // 2) // LANES
    if blocks % 2 == 0:
        blocks -= 1
    return blocks * LANES


def _attn_kernel(with_latent_keys, n_cast, q_ref, kc_ref, vc_ref, *rest):
    n_in = 2 if with_latent_keys else 0
    cast_in = rest[n_in:n_in + n_cast]
    o_ref = rest[n_in + n_cast]
    cast_out = rest[n_in + n_cast + 1:]
    for w_src, w_dst in zip(cast_in, cast_out):
        w_dst[...] = w_src[...].astype(w_dst.dtype)
    if with_latent_keys:
        k = jnp.concatenate([kc_ref[...], rest[0][...]], axis=0)
        v = jnp.concatenate([vc_ref[...], rest[1][...]], axis=0)
    else:
        k, v = kc_ref[...], vc_ref[...]
    v_ones = jnp.concatenate([v, jnp.ones_like(v)], axis=1)
    split = _key_split(k.shape[0])
    for g in range(Q_GROUP):
        cols = slice(g * HEAD_DIM, (g + 1) * HEAD_DIM)
        s = lax.dot_general(q_ref[:, cols], k, (((1,), (1,)), ((), ())), preferred_element_type=F32)
        m = jnp.max(s, axis=-1, keepdims=True)
        if split:
            p_lo = jnp.exp2(s[:, :split] - m).astype(BF)
            p_hi = jnp.exp2(s[:, split:] - m).astype(BF)
            acc = (jnp.dot(p_lo, v_ones[:split], preferred_element_type=F32)
                   + jnp.dot(p_hi, v_ones[split:], preferred_element_type=F32))
        else:
            acc = jnp.dot(jnp.exp2(s - m).astype(BF), v_ones, preferred_element_type=F32)
        o_ref[:, cols] = (acc[:, :HEAD_DIM] / acc[:, HEAD_DIM:]).astype(o_ref.dtype)


def attention(qk, v, qk_ctx, v_ctx, attn_width, with_latent_keys, cast=(), tq=512):
    B, S, _ = qk.shape
    Tc = qk_ctx.shape[1]
    n_kv = v.shape[-1] // HEAD_DIM
    gw = Q_GROUP * HEAD_DIM
    kcol = attn_width // HEAD_DIM
    tq = min(tq, S)
    n_q = S // tq
    n_steps = B * n_kv * n_q
    in_specs = [pl.BlockSpec((None, tq, gw), lambda b, h, i: (b, i, h)),
                pl.BlockSpec((None, Tc, HEAD_DIM), lambda b, h, i: (b, 0, kcol + h)),
                pl.BlockSpec((None, Tc, HEAD_DIM), lambda b, h, i: (b, 0, h))]
    args = [qk, qk_ctx, v_ctx]
    if with_latent_keys:
        in_specs += [pl.BlockSpec((None, S, HEAD_DIM), lambda b, h, i: (b, 0, kcol + h)),
                     pl.BlockSpec((None, S, HEAD_DIM), lambda b, h, i: (b, 0, h))]
        args += [qk, v]
    out_shape = [jax.ShapeDtypeStruct((B, S, attn_width), BF)]
    out_specs = [pl.BlockSpec((None, tq, gw), lambda b, h, i: (b, i, h))]

    def step(b, h, i):
        return (b * n_kv + h) * n_q + i

    for w, w_idx in cast:
        rows, cols = w.shape[-2:]
        slab = rows // n_steps
        assert slab * n_steps == rows and slab % BF16_SUBLANES == 0, (rows, n_steps)
        lead = tuple(w_idx)
        in_specs.append(pl.BlockSpec((None,) * len(lead) + (slab, cols),
                                     lambda b, h, i, lead=lead: lead + (step(b, h, i), 0)))
        args.append(w)
        out_shape.append(jax.ShapeDtypeStruct((rows, cols), BF))
        out_specs.append(pl.BlockSpec((slab, cols), lambda b, h, i: (step(b, h, i), 0)))
    outs = pl.pallas_call(
        functools.partial(_attn_kernel, with_latent_keys, len(cast)),
        out_shape=tuple(out_shape),
        grid=(B, n_kv, n_q),
        in_specs=in_specs,
        out_specs=tuple(out_specs),
        compiler_params=pltpu.CompilerParams(dimension_semantics=("arbitrary",) * 3,
                                             vmem_limit_bytes=ATTENTION_VMEM_LIMIT_BYTES),
        name="attention",
    )(*args)
    return outs[0], list(outs[1:])


def _one_minus_sq(a, y):
    poly = y * (1.0 / 120.0) + (1.0 / 24.0)
    for coef in (1.0 / 6.0, 0.5, 1.0):
        poly = poly * y + coef
    return jnp.where(y > -0.0625, -(poly * y), 1.0 - a * a)


def _tile_scan(a, b, reverse):
    row = lax.broadcasted_iota(jnp.int32, a.shape, 0)
    for d in (1, 2, 4):
        shift = SUBLANES - d if reverse else d
        valid = (row < SUBLANES - d) if reverse else (row >= d)
        a_prev = jnp.where(valid, pltpu.roll(a, shift, 0), 1.0)
        b_prev = jnp.where(valid, pltpu.roll(b, shift, 0), 0.0)
        b = a * b_prev + b
        a = a * a_prev
    return a, b


def _lru_kernel(S, tchunk, unroll, x_ref, gy_ref, h0_ref, cw_ref, cb_ref, wa_ref, ba_ref, wx_ref, bx_ref,
                lam_ref, o_ref, hfin_ref, xs_ref, a_ref, b_ref, h_ref):
    zeros = jnp.zeros((CONV_PAD, LRU_BLOCK), F32)
    xs_ref[0:CONV_PAD, :] = zeros
    xs_ref[CONV_PAD + S:2 * CONV_PAD + S, :] = zeros
    xs_ref[CONV_PAD:CONV_PAD + S, :] = x_ref[...]

    neg_lam = -lam_ref[...]
    softplus = jnp.maximum(neg_lam, 0.0) + jnp.log1p(jnp.exp(-jnp.abs(neg_lam)))
    log_a_per_r = -LRU_C * softplus
    log2_a_per_r = log_a_per_r * math.log2(math.e)

    for c in range(S // tchunk):
        t0 = c * tchunk
        xc = cb_ref[...]
        for j in range(CONV_W):
            xc = xc + xs_ref[pl.ds(CONV_PAD + t0 + j - CONV_LEFT, tchunk), :] * cw_ref[j:j + 1, :]
        xb = xc.astype(BF)
        for d in range(2):
            r = jax.nn.sigmoid(jnp.dot(xb, wa_ref[d], preferred_element_type=F32) + ba_ref[d:d + 1, :])
            i = jax.nn.sigmoid(jnp.dot(xb, wx_ref[d], preferred_element_type=F32) + bx_ref[d:d + 1, :])
            a = jnp.exp2(r * log2_a_per_r[d:d + 1, :])
            a_ref[d, pl.ds(t0, tchunk), :] = a
            one_minus_a2 = _one_minus_sq(a, r * (2.0 * log_a_per_r[d:d + 1, :]))
            b_ref[d, pl.ds(t0, tchunk), :] = jnp.sqrt(one_minus_a2) * (i * xc)

    n_tiles = S // SUBLANES

    def body(i, carry):
        hf, hr = carry
        tf = pl.multiple_of(i * SUBLANES, SUBLANES)
        af, bf = _tile_scan(a_ref[0, pl.ds(tf, SUBLANES), :], b_ref[0, pl.ds(tf, SUBLANES), :], False)
        h = bf + af * hf
        h_ref[0, pl.ds(tf, SUBLANES), :] = h
        hf = h[SUBLANES - 1:SUBLANES, :]
        tr = pl.multiple_of((n_tiles - 1 - i) * SUBLANES, SUBLANES)
        ar, br = _tile_scan(a_ref[1, pl.ds(tr, SUBLANES), :], b_ref[1, pl.ds(tr, SUBLANES), :], True)
        h = br + ar * hr
        h_ref[1, pl.ds(tr, SUBLANES), :] = h
        hr = h[0:1, :]
        return hf, hr

    hf, hr = lax.fori_loop(0, n_tiles, body, (h0_ref[0:1, :], h0_ref[1:2, :]), unroll=unroll)
    hfin_ref[0:1, :] = hf
    hfin_ref[1:2, :] = hr
    o_ref[...] = ((h_ref[0] + h_ref[1]) * gy_ref[...]).astype(o_ref.dtype)


def lru_mix(xseg, gy, h0, conv_w, conv_b, wa, ba, wx, bx, lam, l, tchunk=512, unroll=4):
    B, S, W = xseg.shape
    tchunk = min(tchunk, S)
    nblk = W // LRU_BLOCK
    seq = pl.BlockSpec((None, S, LRU_BLOCK), lambda b, n: (b, 0, n))
    st = pl.BlockSpec((None, 2, LRU_BLOCK), lambda b, n: (b, 0, n))
    vec2 = pl.BlockSpec((None, 2, LRU_BLOCK), lambda b, n: (l, 0, n))
    wblk = pl.BlockSpec((None, 2, None, LRU_BLOCK, LRU_BLOCK), lambda b, n: (l, 0, n, 0, 0))
    return pl.pallas_call(
        functools.partial(_lru_kernel, S, tchunk, unroll),
        out_shape=(jax.ShapeDtypeStruct((B, S, W), BF), jax.ShapeDtypeStruct((B, 2, W), F32)),
        grid=(B, nblk),
        in_specs=[seq, seq, st,
                  pl.BlockSpec((None, CONV_W, LRU_BLOCK), lambda b, n: (l, 0, n)),
                  pl.BlockSpec((None, 1, LRU_BLOCK), lambda b, n: (l, 0, n)),
                  wblk, vec2, wblk, vec2, vec2],
        out_specs=(seq, st),
        scratch_shapes=[pltpu.VMEM((S + 2 * CONV_PAD, LRU_BLOCK), F32),
                        pltpu.VMEM((2, S, LRU_BLOCK), F32),
                        pltpu.VMEM((2, S, LRU_BLOCK), F32),
                        pltpu.VMEM((2, S, LRU_BLOCK), F32)],
        compiler_params=_cparams("arbitrary", "arbitrary"),
        name="lru_mix",
    )(xseg, gy, h0, conv_w, conv_b.reshape(conv_b.shape[0], 1, W), wa, ba, wx, bx, lam)


def _dft_tables(n):
    k = jnp.arange(n, dtype=jnp.int32)
    ang = ((k[:, None] * k[None, :]) % n).astype(F32) * (2.0 * math.pi / n)
    return jnp.cos(ang), jnp.sin(ang)


def _chan_dft_kernel(f_ref, cs_ref, zc_ref, zs_ref):
    w = zc_ref.shape[-1]
    z = jnp.dot(f_ref[...], cs_ref[...], preferred_element_type=F32)
    zc_ref[...] = z[:, :w].astype(zc_ref.dtype)
    zs_ref[...] = z[:, w:].astype(zs_ref.dtype)


def chan_dft(f, cs, tm, out_dtype):
    G, S, W = f.shape
    gw = W // FNET_GROUPS
    blk = pl.BlockSpec((None, tm, gw), lambda g, m, n: (g, m, n))
    return pl.pallas_call(
        _chan_dft_kernel,
        out_shape=(jax.ShapeDtypeStruct((G, S, W), out_dtype),) * 2,
        grid=(G, S // tm, FNET_GROUPS),
        in_specs=[blk, pl.BlockSpec((gw, 2 * gw), lambda g, m, n: (0, 0))],
        out_specs=(blk, blk),
        compiler_params=_cparams("arbitrary", "arbitrary", "arbitrary"),
        name="chan_dft",
    )(f, cs)


def _time_dft_kernel(scale, c_ref, s_ref, zc_ref, zs_ref, o_ref):
    re = jnp.dot(c_ref[...], zc_ref[...], preferred_element_type=F32)
    re = re - jnp.dot(s_ref[...], zs_ref[...], preferred_element_type=F32)
    o_ref[...] = (re * scale).astype(o_ref.dtype)


def time_dft_dense(zc, zs, scale, tm=512, tn=512):
    B, T, W = zc.shape
    tm = min(tm, T)
    ct, st = (t.astype(BF) for t in _dft_tables(T))
    a_spec = pl.BlockSpec((tm, T), lambda b, n, m: (m, 0))
    z_spec = pl.BlockSpec((None, T, tn), lambda b, n, m: (b, 0, n))
    return pl.pallas_call(
        functools.partial(_time_dft_kernel, scale),
        out_shape=jax.ShapeDtypeStruct((B, T, W), BF),
        grid=(B, W // tn, T // tm),
        in_specs=[a_spec, a_spec, z_spec, z_spec],
        out_specs=pl.BlockSpec((None, tm, tn), lambda b, n, m: (b, m, n)),
        compiler_params=_cparams("arbitrary", "arbitrary", "arbitrary"),
        name="time_dft",
    )(ct, st, zc, zs)


def _dft_stage1_kernel(zc_ref, zs_ref, rot_ref, twc_ref, tws_ref, oc_ref, os_ref):
    n1 = zc_ref.shape[0]
    tn = zc_ref.shape[-1]
    rot = rot_ref[...]
    for j in range(zc_ref.shape[1]):
        z = jnp.concatenate([zc_ref[:, j, :], zs_ref[:, j, :]], axis=0).astype(BF)
        res = jnp.dot(rot, z, preferred_element_type=F32)
        ac, as_ = res[:n1], res[n1:]
        twc = jnp.tile(twc_ref[j], (1, tn // LANES))
        tws = jnp.tile(tws_ref[j], (1, tn // LANES))
        oc_ref[j] = (ac * twc - as_ * tws).astype(oc_ref.dtype)
        os_ref[j] = (as_ * twc + ac * tws).astype(os_ref.dtype)


def dft_stage1(zc, zs, n1, n2, tn=512):
    B, T, W = zc.shape
    c, s = _dft_tables(n1)
    rot = jnp.concatenate([jnp.concatenate([c, -s], axis=1), jnp.concatenate([s, c], axis=1)], axis=0).astype(BF)
    cc = jnp.arange(n1, dtype=jnp.int32)
    bb = jnp.arange(n2, dtype=jnp.int32)
    ang = (bb[:, None] * cc[None, :]).astype(F32) * (2.0 * math.pi / T)
    twc = jnp.broadcast_to(jnp.cos(ang)[:, :, None], (n2, n1, LANES))
    tws = jnp.broadcast_to(jnp.sin(ang)[:, :, None], (n2, n1, LANES))
    z_spec = pl.BlockSpec((None, n1, SUBLANES, tn), lambda b, i, n: (b, 0, i, n))
    tw_spec = pl.BlockSpec((SUBLANES, n1, LANES), lambda b, i, n: (i, 0, 0))
    o_spec = pl.BlockSpec((None, SUBLANES, n1, tn), lambda b, i, n: (b, i, 0, n))
    return pl.pallas_call(
        _dft_stage1_kernel,
        out_shape=(jax.ShapeDtypeStruct((B, n2, n1, W), BF),) * 2,
        grid=(B, n2 // SUBLANES, W // tn),
        in_specs=[z_spec, z_spec, pl.BlockSpec((2 * n1, 2 * n1), lambda b, i, n: (0, 0)), tw_spec, tw_spec],
        out_specs=(o_spec, o_spec),
        compiler_params=_cparams("arbitrary", "arbitrary", "arbitrary"),
        name="dft_stage1",
    )(zc.reshape(B, n1, n2, W), zs.reshape(B, n1, n2, W), rot, twc, tws)


def _dft_stage2_kernel(scale, ac_ref, as_ref, cs_ref, o_ref, ac_f32, as_f32, o_f32):
    cs = cs_ref[...]
    ac_f32[...] = ac_ref[...].astype(F32)
    as_f32[...] = as_ref[...].astype(F32)
    for j in range(ac_ref.shape[1]):
        a = jnp.concatenate([ac_f32[:, j, :], as_f32[:, j, :]], axis=0).astype(BF)
        o_f32[:, j, :] = jnp.dot(cs, a, preferred_element_type=F32) * scale
    o_ref[...] = o_f32[...].astype(o_ref.dtype)


def dft_stage2(ac, as_, scale, tn=512):
    B, n2, n1, W = ac.shape
    c, s = _dft_tables(n2)
    cs = jnp.concatenate([c, -s], axis=1).astype(BF)
    cblk = BF16_SUBLANES
    a_spec = pl.BlockSpec((None, n2, cblk, tn), lambda b, i, n: (b, 0, i, n))
    out = pl.pallas_call(
        functools.partial(_dft_stage2_kernel, scale),
        out_shape=jax.ShapeDtypeStruct((B, n2, n1, W), BF),
        grid=(B, n1 // cblk, W // tn),
        in_specs=[a_spec, a_spec, pl.BlockSpec((n2, 2 * n2), lambda b, i, n: (0, 0))],
        out_specs=a_spec,
        scratch_shapes=[pltpu.VMEM((n2, cblk, tn), F32)] * 3,
        compiler_params=_cparams("arbitrary", "arbitrary", "arbitrary"),
        name="dft_stage2",
    )(ac, as_, cs)
    return out.reshape(B, n2 * n1, W)


def _two_stage_dft_ok(T):
    n = math.isqrt(T)
    return n * n == T and n % BF16_SUBLANES == 0


def fourier_branch(f, cs_chan, tm, B, T, scale):
    W = f.shape[-1]
    if _two_stage_dft_ok(T):
        n = math.isqrt(T)
        zc, zs = chan_dft(f, cs_chan, tm, F32)
        ac, as_ = dft_stage1(zc.reshape(B, T, W), zs.reshape(B, T, W), n, n)
        return dft_stage2(ac, as_, scale)
    zc, zs = chan_dft(f, cs_chan, tm, BF)
    return time_dft_dense(zc.reshape(B, T, W), zs.reshape(B, T, W), scale)


def _merge_kernel(b0_ref, b1_ref, b2_ref, w_ref, g0_ref, g1_ref, g2_ref, o_ref):
    acc = g0_ref[...] * _dot(b0_ref[...], w_ref[0])
    acc = acc + g1_ref[...] * _dot(b1_ref[...], w_ref[1])
    acc = acc + g2_ref[...] * _dot(b2_ref[...], w_ref[2])
    o_ref[...] = acc.astype(o_ref.dtype)


def gated_merge(branches, w_branch, w_idx, gates, tm):
    G, S, K = branches[0].shape
    tm = min(tm, 1024)
    tn = 256
    D = w_branch.shape[-1]
    nb = D // tn
    b_spec = pl.BlockSpec((None, tm, K), lambda g, m, n: (g, m, 0))

    def g_spec(j):
        return pl.BlockSpec((None, tm, tn), lambda g, m, n: (g, m, j * nb + n))

    return pl.pallas_call(
        _merge_kernel,
        out_shape=jax.ShapeDtypeStruct((G, S, D), BF),
        grid=(G, S // tm, nb),
        in_specs=[b_spec, b_spec, b_spec,
                  _w_spec(w_idx, (N_BRANCH, K, tn), lambda n: n),
                  g_spec(0), g_spec(1), g_spec(2)],
        out_specs=pl.BlockSpec((None, tm, tn), lambda g, m, n: (g, m, n)),
        compiler_params=_cparams("arbitrary", "arbitrary", "arbitrary"),
        name="gated_merge",
    )(*branches, w_branch, gates, gates, gates)


def _rope_tables(n_tokens):
    rows = n_tokens // GRID_W
    row = jnp.repeat(jnp.arange(rows, dtype=F32), GRID_W)
    col = jnp.tile(jnp.arange(GRID_W, dtype=F32), rows)
    axis_dim = HEAD_DIM // 2
    inv_freq = ROPE_THETA ** (-jnp.arange(0, axis_dim, 2, dtype=F32) / axis_dim)
    ar, ac = row[:, None] * inv_freq, col[:, None] * inv_freq
    cos = jnp.concatenate([jnp.cos(ar), jnp.cos(ar), jnp.cos(ac), jnp.cos(ac)], axis=-1)
    sin = jnp.concatenate([-jnp.sin(ar), jnp.sin(ar), -jnp.sin(ac), jnp.sin(ac)], axis=-1)
    return cos, sin


def kernel(x, c, ctx, c_ctx, w_ada, b_ada, g_norm, w_ff_in, w_ff_out, w_in, b_gate, q_gain, k_gain,
           conv_w, conv_b, lru_wa, lru_ba, lru_wx, lru_bx, lru_lam, w_branch, w_out, g_final):
    B, S, D = x.shape
    Tc = ctx.shape[1]
    depth = w_ada.shape[0]
    lru_w = conv_w.shape[-1]
    fnet_w = D // 2
    attn_w = (D // 256) * HEAD_DIM
    kv_w = attn_w // Q_GROUP
    v0 = attn_w + kv_w
    x0 = v0 + kv_w
    y0 = x0 + lru_w
    f0 = y0 + lru_w
    g0 = f0 + fnet_w

    lru_wa_b = lru_wa.astype(BF)
    lru_wx_b = lru_wx.astype(BF)

    n_rows = SUBLANES
    c_rows = jnp.concatenate([c, c_ctx[None, :], jnp.zeros((n_rows - B - 1, D), F32)], axis=0)
    mod = ada_mod(c_rows, w_ada, b_ada).reshape(depth, n_rows, N_MOD, 1, D)
    mod_lat = mod[:, :B]
    mod_ctx = mod[:, B:B + 1]

    cos_l, sin_l = _rope_tables(S)
    cos_c, sin_c = jnp.ones((B * Tc, HEAD_DIM), F32), jnp.zeros((B * Tc, HEAD_DIM), F32)
    q_fold = HEAD_DIM ** -0.5 * math.log2(math.e)
    gain_qk = jnp.concatenate([jnp.tile(q_gain * q_fold, (1, attn_w // HEAD_DIM)),
                               jnp.tile(k_gain, (1, kv_w // HEAD_DIM))], axis=1)
    gw = fnet_w // FNET_GROUPS
    cg, sg = _dft_tables(gw)
    cs_chan = jnp.concatenate([cg, sg], axis=1).astype(BF)

    tm_ffn = min(2048, S)
    tm_l = min(1024, S)
    tm_c = B * Tc
    xl = x
    xc = ctx.reshape(1, B * Tc, D)
    mods = {"lat": mod_lat, "ctx": mod_ctx}

    wts = {}
    for l in range(depth):
        wts["branch", l] = (w_branch, (l,))
        wts["out", l] = (w_out, (l,))
        for i in range(2):
            wts["ff_in", l, i] = (w_ff_in, (l, i))
            wts["ff_out", l, i] = (w_ff_out, (l, i))
    w_branch_rows = w_branch.reshape(depth, N_BRANCH * w_branch.shape[2], D)

    def ffn(xs, which, l, i, k_mod):
        lat = which == "lat"
        mod_l = mods[which][l]
        h = norm_mod(xs, g_norm[l, 2 * i], mod_l, k_mod, k_mod + 1)
        act = swiglu_in(h, *wts["ff_in", l, i], tm_ffn if lat else tm_c)
        return resid_out(act, *wts["ff_out", l, i], xs, mod_l, k_mod + 2, 0.5, tm_l if lat else tm_c)

    for l in range(depth):
        last = l == depth - 1
        xl = ffn(xl, "lat", l, 0, 0)
        xc = ffn(xc, "ctx", l, 0, 0)

        hl = norm_mod(xl, g_norm[l, 1], mod_lat[l], 3, 4)
        hc = norm_mod(xc, g_norm[l, 1], mod_ctx[l], 3, 4)
        gain = gain_qk[l:l + 1]
        bias_g = b_gate[l].reshape(1, N_BRANCH * D)

        qk_c = proj_qk(hc, w_in, l, attn_w + kv_w, gain, cos_c, sin_c, B * Tc, tm_c).reshape(B, Tc, -1)
        v_c = proj(hc, w_in, l, v0, kv_w, BF, tm_c).reshape(B, Tc, kv_w)
        xs_c = proj(hc, w_in, l, x0, lru_w, F32, tm_c).reshape(B, Tc, lru_w)
        if last:
            gy_c = jnp.zeros((B, Tc, lru_w), F32)
        else:
            gy_c = proj(hc, w_in, l, y0, lru_w, F32, tm_c, act="gelu").reshape(B, Tc, lru_w)
        lru_args = (conv_w, conv_b, lru_wa_b, lru_ba, lru_wx_b, lru_bx, lru_lam, l)
        lru_c, h_fin = lru_mix(xs_c, gy_c, jnp.zeros((B, 2, lru_w), F32), *lru_args)

        qk_l = proj_qk(hl, w_in, l, attn_w + kv_w, gain, cos_l, sin_l, S, tm_l)
        v_l, xs_l, gy_l, f_l = proj_segments(
            hl, w_in, l, v0, ((kv_w, BF, "none"), (lru_w, F32, "none"), (lru_w, F32, "gelu"), (fnet_w, BF, "none")),
            tm_l)
        gate_l = proj(hl, w_in, l, g0, N_BRANCH * D, F32, tm_l, act="sigmoid_bias", bias=bias_g)

        cast_keys = [("branch", l), ("out", l), ("ff_in", l, 1), ("ff_out", l, 1)]
        if not last:
            cast_keys += [("ff_in", l + 1, 0), ("ff_out", l + 1, 0)]
        cast = [(w_branch_rows, (l,)) if key[0] == "branch" else wts[key] for key in cast_keys]
        attn_l, copies = attention(qk_l, v_l, qk_c, v_c, attn_w, True, cast=cast)
        for key, w_bf in zip(cast_keys, copies):
            if key[0] == "branch":
                w_bf = w_bf.reshape(w_branch.shape[1:])
            wts[key] = (w_bf, ())

        lru_l, _ = lru_mix(xs_l, gy_l, h_fin, *lru_args)
        four_l = fourier_branch(f_l, cs_chan, tm_l, B, S, 1.0 / math.sqrt(S * gw))
        merged = gated_merge((attn_l, lru_l, four_l), *wts["branch", l], gate_l, tm_l)
        xl = resid_out(merged, *wts["out", l], xl, mod_lat[l], 5, 1.0, tm_l)
        xl = ffn(xl, "lat", l, 1, 6)

        if not last:
            f_c = proj(hc, w_in, l, f0, fnet_w, BF, tm_c)
            gate_c = proj(hc, w_in, l, g0, N_BRANCH * D, F32, tm_c, act="sigmoid_bias", bias=bias_g)
            attn_c, _ = attention(qk_c, v_c, qk_c, v_c, attn_w, False)
            four_c = fourier_branch(f_c, cs_chan, tm_c, B, Tc, 1.0 / math.sqrt(Tc * gw))
            merged_c = gated_merge((attn_c.reshape(1, B * Tc, attn_w), lru_c.reshape(1, B * Tc, lru_w),
                                    four_c.reshape(1, B * Tc, fnet_w)), *wts["branch", l], gate_c, tm_c)
            xc = resid_out(merged_c, *wts["out", l], xc, mod_ctx[l], 5, 1.0, tm_c)
            xc = ffn(xc, "ctx", l, 1, 6)

    return final_norm(xl, g_final)
```

```python
import functools
import math

import jax
import jax.numpy as jnp
from jax import lax
from jax.experimental import pallas as pl
from jax.experimental.pallas import tpu as pltpu

F32 = jnp.float32
BF = jnp.bfloat16

VMEM_LIMIT_BYTES = 56 * 1024 * 1024
ATTENTION_VMEM_LIMIT_BYTES = 60 * 1024 * 1024
LANES = 128
SUBLANES = 8
BF16_SUBLANES = 16

HEAD_DIM = 128
Q_GROUP = 4
GRID_W = 64
CONV_W = 4
CONV_LEFT = 2
LRU_C = 8.0
LRU_BLOCK = 128
FNET_GROUPS = 4
N_BRANCH = 3
N_MOD = 9
ROPE_THETA = 10000.0
EPS = 1e-6
CONV_PAD = 8


def _cparams(*sem):
    return pltpu.CompilerParams(dimension_semantics=sem, vmem_limit_bytes=VMEM_LIMIT_BYTES)


def _silu(a):
    return a * jax.nn.sigmoid(a)


MATMUL_VMEM_BUDGET = 55 * 1024 * 1024


def _plan_tiles(tm_max, K, w_dtype, w_tiles, io_bytes_per_elem, tmp_bytes_per_elem, extra_bytes_per_row=0,
                tn_options=(512, 256)):
    w_bytes = jnp.dtype(w_dtype).itemsize
    w_copy_bytes = 0 if w_dtype == BF else 2
    tm = tm_max
    while tm >= 256:
        for tn in tn_options:
            need = (2 * tm * K * 2 + w_tiles * K * tn * (2 * w_bytes + w_copy_bytes)
                    + tm * tn * (2 * io_bytes_per_elem + tmp_bytes_per_elem) + tm * extra_bytes_per_row)
            if need <= MATMUL_VMEM_BUDGET:
                return tm, tn
        tm //= 2
    raise ValueError("no matmul tiling fits VMEM")


def _lhs_spec(tm, K):
    return pl.BlockSpec((None, tm, K), lambda g, m, n: (g, m, 0))


def _dot(a, w):
    return jnp.dot(a, w.astype(BF), preferred_element_type=F32)


EPILOGUE_ROWS = 256


def _row_chunks(tm):
    rows = min(EPILOGUE_ROWS, tm)
    return [slice(c * rows, (c + 1) * rows) for c in range(tm // rows)]


def _ada_kernel(c_ref, w_ref, b_ref, o_ref):
    s = _silu(c_ref[...]).astype(BF)
    o_ref[...] = _dot(s, w_ref[...]) + b_ref[...]


def ada_mod(c_rows, w_ada, b_ada, tn=1024):
    depth, d, n = w_ada.shape
    rows = c_rows.shape[0]
    return pl.pallas_call(
        _ada_kernel,
        out_shape=jax.ShapeDtypeStruct((depth, rows, n), F32),
        grid=(depth, n // tn),
        in_specs=[pl.BlockSpec((rows, d), lambda l, j: (0, 0)),
                  pl.BlockSpec((None, d, tn), lambda l, j: (l, 0, j)),
                  pl.BlockSpec((None, 1, tn), lambda l, j: (l, 0, j))],
        out_specs=pl.BlockSpec((None, rows, tn), lambda l, j: (l, 0, j)),
        compiler_params=_cparams("arbitrary", "arbitrary"),
        name="ada_mod",
    )(c_rows, w_ada, b_ada.reshape(depth, 1, n))


def _norm_mod_kernel(x_ref, g_ref, sh_ref, sc_ref, o_ref):
    x = x_ref[...]
    ms = jnp.mean(x * x, axis=-1, keepdims=True)
    y = x_ref[...] * lax.rsqrt(ms + EPS) * g_ref[...]
    o_ref[...] = (y * (1.0 + sc_ref[...]) + sh_ref[...]).astype(o_ref.dtype)


def norm_mod(x, g, mod, k_shift, k_scale, ts=512):
    G, S, D = x.shape
    ts = min(ts, S)
    return pl.pallas_call(
        _norm_mod_kernel,
        out_shape=jax.ShapeDtypeStruct((G, S, D), BF),
        grid=(G, S // ts),
        in_specs=[pl.BlockSpec((None, ts, D), lambda g_, i: (g_, i, 0)),
                  pl.BlockSpec((1, D), lambda g_, i: (0, 0)),
                  pl.BlockSpec((None, None, 1, D), lambda g_, i: (g_, k_shift, 0, 0)),
                  pl.BlockSpec((None, None, 1, D), lambda g_, i: (g_, k_scale, 0, 0))],
        out_specs=pl.BlockSpec((None, ts, D), lambda g_, i: (g_, i, 0)),
        compiler_params=_cparams("arbitrary", "arbitrary"),
        name="norm_mod",
    )(x, g.reshape(1, D), mod, mod)


def _final_norm_kernel(x_ref, g_ref, o_ref):
    x = x_ref[...]
    ms = jnp.mean(x * x, axis=-1, keepdims=True)
    o_ref[...] = x * lax.rsqrt(ms + EPS) * g_ref[...]


def final_norm(x, g, ts=512):
    G, S, D = x.shape
    ts = min(ts, S)
    return pl.pallas_call(
        _final_norm_kernel,
        out_shape=jax.ShapeDtypeStruct((G, S, D), F32),
        grid=(G, S // ts),
        in_specs=[pl.BlockSpec((None, ts, D), lambda g_, i: (g_, i, 0)),
                  pl.BlockSpec((1, D), lambda g_, i: (0, 0))],
        out_specs=pl.BlockSpec((None, ts, D), lambda g_, i: (g_, i, 0)),
        compiler_params=_cparams("arbitrary", "arbitrary"),
        name="final_norm",
    )(x, g.reshape(1, D))


def _swiglu_kernel(h_ref, wa_ref, wb_ref, o_ref):
    wa = wa_ref[...].astype(BF)
    wb = wb_ref[...].astype(BF)
    for r in _row_chunks(h_ref.shape[0]):
        h = h_ref[r, :]
        a = _dot(h, wa)
        b = _dot(h, wb)
        o_ref[r, :] = (_silu(a) * b).astype(o_ref.dtype)


def _w_spec(w_idx, block, col_block):
    lead = tuple(w_idx)
    return pl.BlockSpec((None,) * len(lead) + block,
                        lambda g, m, n: lead + (0,) * (len(block) - 1) + (col_block(n),))


def swiglu_in(h, w, w_idx, tm):
    G, S, K = h.shape
    F = w.shape[-1] // 2
    tm, tn = _plan_tiles(tm, K, w.dtype, 2, 2, 0, tn_options=(256,))
    nb = F // tn
    return pl.pallas_call(
        _swiglu_kernel,
        out_shape=jax.ShapeDtypeStruct((G, S, F), BF),
        grid=(G, S // tm, nb),
        in_specs=[_lhs_spec(tm, K),
                  _w_spec(w_idx, (K, tn), lambda n: n),
                  _w_spec(w_idx, (K, tn), lambda n: n + nb)],
        out_specs=pl.BlockSpec((None, tm, tn), lambda g, m, n: (g, m, n)),
        compiler_params=_cparams("arbitrary", "arbitrary", "arbitrary"),
        name="swiglu_in",
    )(h, w, w)


def _resid_kernel(coef, a_ref, w_ref, x_ref, g_ref, o_ref):
    y = _dot(a_ref[...], w_ref[...])
    o_ref[...] = x_ref[...] + (coef * g_ref[...]) * y


def resid_out(a, w, w_idx, x, mod, k_gate, coef, tm):
    G, S, K = a.shape
    D = x.shape[-1]
    tm, tn = _plan_tiles(tm, K, w.dtype, 1, 8, 16)
    return pl.pallas_call(
        functools.partial(_resid_kernel, coef),
        out_shape=jax.ShapeDtypeStruct((G, S, D), F32),
        grid=(G, S // tm, D // tn),
        in_specs=[_lhs_spec(tm, K),
                  _w_spec(w_idx, (K, tn), lambda n: n),
                  pl.BlockSpec((None, tm, tn), lambda g, m, n: (g, m, n)),
                  pl.BlockSpec((None, None, 1, tn), lambda g, m, n: (g, k_gate, 0, n))],
        out_specs=pl.BlockSpec((None, tm, tn), lambda g, m, n: (g, m, n)),
        compiler_params=_cparams("arbitrary", "arbitrary", "arbitrary"),
        name="resid_out",
    )(a, w, x, mod)


def _proj_kernel(act, h_ref, w_ref, *rest):
    o_ref = rest[-1]
    w = w_ref[...].astype(BF)
    for r in _row_chunks(h_ref.shape[0]):
        acc = _dot(h_ref[r, :], w)
        if act == "gelu":
            acc = jax.nn.gelu(acc)
        elif act == "sigmoid_bias":
            acc = jax.nn.sigmoid(acc + rest[0][...])
        o_ref[r, :] = acc.astype(o_ref.dtype)


def proj(h, w_in, l, col0, ncols, out_dtype, tm, act="none", bias=None):
    G, S, K = h.shape
    tm, tn = _plan_tiles(tm, K, w_in.dtype, 1, jnp.dtype(out_dtype).itemsize, 8)
    cb = col0 // tn
    in_specs = [_lhs_spec(tm, K),
                pl.BlockSpec((None, K, tn), lambda g, m, n: (l, 0, cb + n))]
    args = [h, w_in]
    if bias is not None:
        in_specs.append(pl.BlockSpec((1, tn), lambda g, m, n: (0, n)))
        args.append(bias)
    return pl.pallas_call(
        functools.partial(_proj_kernel, act),
        out_shape=jax.ShapeDtypeStruct((G, S, ncols), out_dtype),
        grid=(G, S // tm, ncols // tn),
        in_specs=in_specs,
        out_specs=pl.BlockSpec((None, tm, tn), lambda g, m, n: (g, m, n)),
        compiler_params=_cparams("arbitrary", "arbitrary", "arbitrary"),
        name="proj_" + act,
    )(*args)


def _proj_segments_kernel(segments, h_ref, w_ref, bias_ref, *o_refs):
    n = pl.program_id(2)
    for (lo, hi, act), o_ref in zip(segments, o_refs):
        @pl.when((n >= lo) & (n < hi))
        def _(act=act, o_ref=o_ref):
            w = w_ref[...].astype(BF)
            for r in _row_chunks(h_ref.shape[0]):
                acc = _dot(h_ref[r, :], w)
                if act == "gelu":
                    acc = jax.nn.gelu(acc)
                elif act == "sigmoid_bias":
                    acc = jax.nn.sigmoid(acc + bias_ref[...])
                o_ref[r, :] = acc.astype(o_ref.dtype)


def proj_segments(h, w_in, l, col0, segments, bias, tm, tn=512):
    G, S, K = h.shape
    bounds, lo = [], 0
    for ncols, _, act in segments:
        bounds.append((lo, lo + ncols // tn, act))
        if act == "sigmoid_bias":
            bias_lo, bias_tiles = lo, ncols // tn
        lo += ncols // tn
    cb = col0 // tn

    def out_spec(lo, hi):
        return pl.BlockSpec((None, tm, tn), lambda g, m, n: (g, m, jnp.clip(n - lo, 0, hi - lo - 1)))

    return pl.pallas_call(
        functools.partial(_proj_segments_kernel, tuple(bounds)),
        out_shape=tuple(jax.ShapeDtypeStruct((G, S, ncols), dt) for ncols, dt, _ in segments),
        grid=(G, S // tm, lo),
        in_specs=[_lhs_spec(tm, K),
                  pl.BlockSpec((None, K, tn), lambda g, m, n: (l, 0, cb + n)),
                  pl.BlockSpec((1, tn), lambda g, m, n: (0, jnp.clip(n - bias_lo, 0, bias_tiles - 1)))],
        out_specs=tuple(out_spec(a, b) for a, b, _ in bounds),
        compiler_params=_cparams("arbitrary", "arbitrary", "arbitrary"),
        name="proj_segments",
    )(h, w_in, bias)


def _qk_kernel(h_ref, w_ref, gain_ref, cos_ref, sin_ref, swap_ref, o_ref):
    acc = _dot(h_ref[...], w_ref[...])
    cos = cos_ref[...]
    sin = sin_ref[...]
    swap = swap_ref[...]
    for hd in range(acc.shape[1] // HEAD_DIM):
        cols = slice(hd * HEAD_DIM, (hd + 1) * HEAD_DIM)
        v = acc[:, cols]
        ms = jnp.mean(v * v, axis=-1, keepdims=True)
        y = v * lax.rsqrt(ms + EPS) * gain_ref[:, cols]
        hi = y.astype(BF)
        lo = (y - hi.astype(F32)).astype(BF)
        partner = (jnp.dot(hi, swap, preferred_element_type=F32)
                   + jnp.dot(lo, swap, preferred_element_type=F32))
        o_ref[:, cols] = (y * cos + partner * sin).astype(o_ref.dtype)


def proj_qk(h, w_in, l, ncols, gain, cos, sin, rows_per_seq, tm):
    G, S, K = h.shape
    tm, tn = _plan_tiles(tm, K, w_in.dtype, 1, 2, 8, extra_bytes_per_row=2 * 2 * HEAD_DIM * 4)
    seq_blocks = rows_per_seq // tm
    lane = jnp.arange(HEAD_DIM)
    quarter = HEAD_DIM // 4
    partner_lane = jnp.where((lane % (2 * quarter)) < quarter, lane + quarter, lane - quarter)
    swap = (lane[:, None] == partner_lane[None, :]).astype(BF)
    return pl.pallas_call(
        _qk_kernel,
        out_shape=jax.ShapeDtypeStruct((G, S, ncols), BF),
        grid=(G, S // tm, ncols // tn),
        in_specs=[_lhs_spec(tm, K),
                  pl.BlockSpec((None, K, tn), lambda g, m, n: (l, 0, n)),
                  pl.BlockSpec((1, tn), lambda g, m, n: (0, n)),
                  pl.BlockSpec((tm, HEAD_DIM), lambda g, m, n: (m % seq_blocks, 0)),
                  pl.BlockSpec((tm, HEAD_DIM), lambda g, m, n: (m % seq_blocks, 0)),
                  pl.BlockSpec((HEAD_DIM, HEAD_DIM), lambda g, m, n: (0, 0))],
        out_specs=pl.BlockSpec((None, tm, tn), lambda g, m, n: (g, m, n)),
        compiler_params=_cparams("arbitrary", "arbitrary", "arbitrary"),
        name="proj_qk",
    )(h, w_in, gain, cos, sin, swap)


def _key_split(n_keys):
    if n_keys < 3 * LANES:
        return 0
    blocks = (n_keys // 2) // LANES
    if blocks % 2 == 0:
        blocks -= 1
    return blocks * LANES


def _attn_kernel(with_latent_keys, n_cast, q_ref, kc_ref, vc_ref, *rest):
    n_in = 2 if with_latent_keys else 0
    cast_in = rest[n_in:n_in + n_cast]
    o_ref = rest[n_in + n_cast]
    cast_out = rest[n_in + n_cast + 1:]
    for w_src, w_dst in zip(cast_in, cast_out):
        w_dst[...] = w_src[...].astype(w_dst.dtype)
    if with_latent_keys:
        k = jnp.concatenate([kc_ref[...], rest[0][...]], axis=0)
        v = jnp.concatenate([vc_ref[...], rest[1][...]], axis=0)
    else:
        k, v = kc_ref[...], vc_ref[...]
    v_ones = jnp.concatenate([v, jnp.ones_like(v)], axis=1)
    split = _key_split(k.shape[0])
    for g in range(Q_GROUP):
        cols = slice(g * HEAD_DIM, (g + 1) * HEAD_DIM)
        s = lax.dot_general(q_ref[:, cols], k, (((1,), (1,)), ((), ())), preferred_element_type=F32)
        m = jnp.max(s, axis=-1, keepdims=True)
        if split:
            p_lo = jnp.exp2(s[:, :split] - m).astype(BF)
            p_hi = jnp.exp2(s[:, split:] - m).astype(BF)
            acc = (jnp.dot(p_lo, v_ones[:split], preferred_element_type=F32)
                   + jnp.dot(p_hi, v_ones[split:], preferred_element_type=F32))
        else:
            acc = jnp.dot(jnp.exp2(s - m).astype(BF), v_ones, preferred_element_type=F32)
        o_ref[:, cols] = (acc[:, :HEAD_DIM] / acc[:, HEAD_DIM:]).astype(o_ref.dtype)


def attention(qk, v, qk_ctx, v_ctx, attn_width, with_latent_keys, cast=(), tq=512):
    B, S, _ = qk.shape
    Tc = qk_ctx.shape[1]
    n_kv = v.shape[-1] // HEAD_DIM
    gw = Q_GROUP * HEAD_DIM
    kcol = attn_width // HEAD_DIM
    tq = min(tq, S)
    n_q = S // tq
    n_steps = B * n_kv * n_q
    in_specs = [pl.BlockSpec((None, tq, gw), lambda b, h, i: (b, i, h)),
                pl.BlockSpec((None, Tc, HEAD_DIM), lambda b, h, i: (b, 0, kcol + h)),
                pl.BlockSpec((None, Tc, HEAD_DIM), lambda b, h, i: (b, 0, h))]
    args = [qk, qk_ctx, v_ctx]
    if with_latent_keys:
        in_specs += [pl.BlockSpec((None, S, HEAD_DIM), lambda b, h, i: (b, 0, kcol + h)),
                     pl.BlockSpec((None, S, HEAD_DIM), lambda b, h, i: (b, 0, h))]
        args += [qk, v]
    out_shape = [jax.ShapeDtypeStruct((B, S, attn_width), BF)]
    out_specs = [pl.BlockSpec((None, tq, gw), lambda b, h, i: (b, i, h))]

    def step(b, h, i):
        return (b * n_kv + h) * n_q + i

    for w, w_idx in cast:
        rows, cols = w.shape[-2:]
        slab = rows // n_steps
        assert slab * n_steps == rows and slab % BF16_SUBLANES == 0, (rows, n_steps)
        lead = tuple(w_idx)
        in_specs.append(pl.BlockSpec((None,) * len(lead) + (slab, cols),
                                     lambda b, h, i, lead=lead: lead + (step(b, h, i), 0)))
        args.append(w)
        out_shape.append(jax.ShapeDtypeStruct((rows, cols), BF))
        out_specs.append(pl.BlockSpec((slab, cols), lambda b, h, i: (step(b, h, i), 0)))
    outs = pl.pallas_call(
        functools.partial(_attn_kernel, with_latent_keys, len(cast)),
        out_shape=tuple(out_shape),
        grid=(B, n_kv, n_q),
        in_specs=in_specs,
        out_specs=tuple(out_specs),
        compiler_params=pltpu.CompilerParams(dimension_semantics=("arbitrary",) * 3,
                                             vmem_limit_bytes=ATTENTION_VMEM_LIMIT_BYTES),
        name="attention",
    )(*args)
    return outs[0], list(outs[1:])


def _one_minus_sq(a, y):
    poly = y * (1.0 / 120.0) + (1.0 / 24.0)
    for coef in (1.0 / 6.0, 0.5, 1.0):
        poly = poly * y + coef
    return jnp.where(y > -0.0625, -(poly * y), 1.0 - a * a)


def _tile_scan(a, b, reverse):
    row = lax.broadcasted_iota(jnp.int32, a.shape, 0)
    for d in (1, 2, 4):
        shift = SUBLANES - d if reverse else d
        valid = (row < SUBLANES - d) if reverse else (row >= d)
        a_prev = jnp.where(valid, pltpu.roll(a, shift, 0), 1.0)
        b_prev = jnp.where(valid, pltpu.roll(b, shift, 0), 0.0)
        b = a * b_prev + b
        a = a * a_prev
    return a, b


def _lru_kernel(S, tchunk, unroll, x_ref, gy_ref, h0_ref, cw_ref, cb_ref, wa_ref, ba_ref, wx_ref, bx_ref,
                lam_ref, o_ref, hfin_ref, xs_ref, a_ref, b_ref, h_ref):
    zeros = jnp.zeros((CONV_PAD, LRU_BLOCK), F32)
    xs_ref[0:CONV_PAD, :] = zeros
    xs_ref[CONV_PAD + S:2 * CONV_PAD + S, :] = zeros
    xs_ref[CONV_PAD:CONV_PAD + S, :] = x_ref[...]

    neg_lam = -lam_ref[...]
    softplus = jnp.maximum(neg_lam, 0.0) + jnp.log1p(jnp.exp(-jnp.abs(neg_lam)))
    log_a_per_r = -LRU_C * softplus
    log2_a_per_r = log_a_per_r * math.log2(math.e)

    for c in range(S // tchunk):
        t0 = c * tchunk
        xc = cb_ref[...]
        for j in range(CONV_W):
            xc = xc + xs_ref[pl.ds(CONV_PAD + t0 + j - CONV_LEFT, tchunk), :] * cw_ref[j:j + 1, :]
        xb = xc.astype(BF)
        for d in range(2):
            r = jax.nn.sigmoid(jnp.dot(xb, wa_ref[d], preferred_element_type=F32) + ba_ref[d:d + 1, :])
            i = jax.nn.sigmoid(jnp.dot(xb, wx_ref[d], preferred_element_type=F32) + bx_ref[d:d + 1, :])
            a = jnp.exp2(r * log2_a_per_r[d:d + 1, :])
            a_ref[d, pl.ds(t0, tchunk), :] = a
            one_minus_a2 = _one_minus_sq(a, r * (2.0 * log_a_per_r[d:d + 1, :]))
            b_ref[d, pl.ds(t0, tchunk), :] = jnp.sqrt(one_minus_a2) * (i * xc)

    n_tiles = S // SUBLANES

    def body(i, carry):
        hf, hr = carry
        tf = pl.multiple_of(i * SUBLANES, SUBLANES)
        af, bf = _tile_scan(a_ref[0, pl.ds(tf, SUBLANES), :], b_ref[0, pl.ds(tf, SUBLANES), :], False)
        h = bf + af * hf
        h_ref[0, pl.ds(tf, SUBLANES), :] = h
        hf = h[SUBLANES - 1:SUBLANES, :]
        tr = pl.multiple_of((n_tiles - 1 - i) * SUBLANES, SUBLANES)
        ar, br = _tile_scan(a_ref[1, pl.ds(tr, SUBLANES), :], b_ref[1, pl.ds(tr, SUBLANES), :], True)
        h = br + ar * hr
        h_ref[1, pl.ds(tr, SUBLANES), :] = h
        hr = h[0:1, :]
        return hf, hr

    hf, hr = lax.fori_loop(0, n_tiles, body, (h0_ref[0:1, :], h0_ref[1:2, :]), unroll=unroll)
    hfin_ref[0:1, :] = hf
    hfin_ref[1:2, :] = hr
    o_ref[...] = ((h_ref[0] + h_ref[1]) * gy_ref[...]).astype(o_ref.dtype)


def lru_mix(xseg, gy, h0, conv_w, conv_b, wa, ba, wx, bx, lam, l, tchunk=512, unroll=4):
    B, S, W = xseg.shape
    tchunk = min(tchunk, S)
    nblk = W // LRU_BLOCK
    seq = pl.BlockSpec((None, S, LRU_BLOCK), lambda b, n: (b, 0, n))
    st = pl.BlockSpec((None, 2, LRU_BLOCK), lambda b, n: (b, 0, n))
    vec2 = pl.BlockSpec((None, 2, LRU_BLOCK), lambda b, n: (l, 0, n))
    wblk = pl.BlockSpec((None, 2, None, LRU_BLOCK, LRU_BLOCK), lambda b, n: (l, 0, n, 0, 0))
    return pl.pallas_call(
        functools.partial(_lru_kernel, S, tchunk, unroll),
        out_shape=(jax.ShapeDtypeStruct((B, S, W), BF), jax.ShapeDtypeStruct((B, 2, W), F32)),
        grid=(B, nblk),
        in_specs=[seq, seq, st,
                  pl.BlockSpec((None, CONV_W, LRU_BLOCK), lambda b, n: (l, 0, n)),
                  pl.BlockSpec((None, 1, LRU_BLOCK), lambda b, n: (l, 0, n)),
                  wblk, vec2, wblk, vec2, vec2],
        out_specs=(seq, st),
        scratch_shapes=[pltpu.VMEM((S + 2 * CONV_PAD, LRU_BLOCK), F32),
                        pltpu.VMEM((2, S, LRU_BLOCK), F32),
                        pltpu.VMEM((2, S, LRU_BLOCK), F32),
                        pltpu.VMEM((2, S, LRU_BLOCK), F32)],
        compiler_params=_cparams("arbitrary", "arbitrary"),
        name="lru_mix",
    )(xseg, gy, h0, conv_w, conv_b.reshape(conv_b.shape[0], 1, W), wa, ba, wx, bx, lam)


def _dft_tables(n):
    k = jnp.arange(n, dtype=jnp.int32)
    ang = ((k[:, None] * k[None, :]) % n).astype(F32) * (2.0 * math.pi / n)
    return jnp.cos(ang), jnp.sin(ang)


def _chan_dft_kernel(f_ref, cs_ref, zc_ref, zs_ref):
    w = zc_ref.shape[-1]
    z = jnp.dot(f_ref[...], cs_ref[...], preferred_element_type=F32)
    zc_ref[...] = z[:, :w].astype(zc_ref.dtype)
    zs_ref[...] = z[:, w:].astype(zs_ref.dtype)


def chan_dft(f, cs, tm, out_dtype):
    G, S, W = f.shape
    gw = W // FNET_GROUPS
    blk = pl.BlockSpec((None, tm, gw), lambda g, m, n: (g, m, n))
    return pl.pallas_call(
        _chan_dft_kernel,
        out_shape=(jax.ShapeDtypeStruct((G, S, W), out_dtype),) * 2,
        grid=(G, S // tm, FNET_GROUPS),
        in_specs=[blk, pl.BlockSpec((gw, 2 * gw), lambda g, m, n: (0, 0))],
        out_specs=(blk, blk),
        compiler_params=_cparams("arbitrary", "arbitrary", "arbitrary"),
        name="chan_dft",
    )(f, cs)


def _time_dft_kernel(scale, c_ref, s_ref, zc_ref, zs_ref, o_ref):
    re = jnp.dot(c_ref[...], zc_ref[...], preferred_element_type=F32)
    re = re - jnp.dot(s_ref[...], zs_ref[...], preferred_element_type=F32)
    o_ref[...] = (re * scale).astype(o_ref.dtype)


def time_dft_dense(zc, zs, scale, tm=512, tn=512):
    B, T, W = zc.shape
    tm = min(tm, T)
    ct, st = (t.astype(BF) for t in _dft_tables(T))
    a_spec = pl.BlockSpec((tm, T), lambda b, n, m: (m, 0))
    z_spec = pl.BlockSpec((None, T, tn), lambda b, n, m: (b, 0, n))
    return pl.pallas_call(
        functools.partial(_time_dft_kernel, scale),
        out_shape=jax.ShapeDtypeStruct((B, T, W), BF),
        grid=(B, W // tn, T // tm),
        in_specs=[a_spec, a_spec, z_spec, z_spec],
        out_specs=pl.BlockSpec((None, tm, tn), lambda b, n, m: (b, m, n)),
        compiler_params=_cparams("arbitrary", "arbitrary", "arbitrary"),
        name="time_dft",
    )(ct, st, zc, zs)


def _dft_stage1_kernel(zc_ref, zs_ref, rot_ref, twc_ref, tws_ref, oc_ref, os_ref):
    n1 = zc_ref.shape[0]
    tn = zc_ref.shape[-1]
    rot = rot_ref[...]
    for j in range(zc_ref.shape[1]):
        z = jnp.concatenate([zc_ref[:, j, :], zs_ref[:, j, :]], axis=0).astype(BF)
        res = jnp.dot(rot, z, preferred_element_type=F32)
        ac, as_ = res[:n1], res[n1:]
        twc = jnp.tile(twc_ref[j], (1, tn // LANES))
        tws = jnp.tile(tws_ref[j], (1, tn // LANES))
        oc_ref[j] = (ac * twc - as_ * tws).astype(oc_ref.dtype)
        os_ref[j] = (as_ * twc + ac * tws).astype(os_ref.dtype)


def dft_stage1(zc, zs, n1, n2, tn=512):
    B, T, W = zc.shape
    c, s = _dft_tables(n1)
    rot = jnp.concatenate([jnp.concatenate([c, -s], axis=1), jnp.concatenate([s, c], axis=1)], axis=0).astype(BF)
    cc = jnp.arange(n1, dtype=jnp.int32)
    bb = jnp.arange(n2, dtype=jnp.int32)
    ang = (bb[:, None] * cc[None, :]).astype(F32) * (2.0 * math.pi / T)
    twc = jnp.broadcast_to(jnp.cos(ang)[:, :, None], (n2, n1, LANES))
    tws = jnp.broadcast_to(jnp.sin(ang)[:, :, None], (n2, n1, LANES))
    z_spec = pl.BlockSpec((None, n1, SUBLANES, tn), lambda b, i, n: (b, 0, i, n))
    tw_spec = pl.BlockSpec((SUBLANES, n1, LANES), lambda b, i, n: (i, 0, 0))
    o_spec = pl.BlockSpec((None, SUBLANES, n1, tn), lambda b, i, n: (b, i, 0, n))
    return pl.pallas_call(
        _dft_stage1_kernel,
        out_shape=(jax.ShapeDtypeStruct((B, n2, n1, W), BF),) * 2,
        grid=(B, n2 // SUBLANES, W // tn),
        in_specs=[z_spec, z_spec, pl.BlockSpec((2 * n1, 2 * n1), lambda b, i, n: (0, 0)), tw_spec, tw_spec],
        out_specs=(o_spec, o_spec),
        compiler_params=_cparams("arbitrary", "arbitrary", "arbitrary"),
        name="dft_stage1",
    )(zc.reshape(B, n1, n2, W), zs.reshape(B, n1, n2, W), rot, twc, tws)


def _dft_stage2_kernel(scale, ac_ref, as_ref, cs_ref, o_ref, ac_f32, as_f32, o_f32):
    cs = cs_ref[...]
    ac_f32[...] = ac_ref[...].astype(F32)
    as_f32[...] = as_ref[...].astype(F32)
    for j in range(ac_ref.shape[1]):
        a = jnp.concatenate([ac_f32[:, j, :], as_f32[:, j, :]], axis=0).astype(BF)
        o_f32[:, j, :] = jnp.dot(cs, a, preferred_element_type=F32) * scale
    o_ref[...] = o_f32[...].astype(o_ref.dtype)


def dft_stage2(ac, as_, scale, tn=512):
    B, n2, n1, W = ac.shape
    c, s = _dft_tables(n2)
    cs = jnp.concatenate([c, -s], axis=1).astype(BF)
    cblk = BF16_SUBLANES
    a_spec = pl.BlockSpec((None, n2, cblk, tn), lambda b, i, n: (b, 0, i, n))
    out = pl.pallas_call(
        functools.partial(_dft_stage2_kernel, scale),
        out_shape=jax.ShapeDtypeStruct((B, n2, n1, W), BF),
        grid=(B, n1 // cblk, W // tn),
        in_specs=[a_spec, a_spec, pl.BlockSpec((n2, 2 * n2), lambda b, i, n: (0, 0))],
        out_specs=a_spec,
        scratch_shapes=[pltpu.VMEM((n2, cblk, tn), F32)] * 3,
        compiler_params=_cparams("arbitrary", "arbitrary", "arbitrary"),
        name="dft_stage2",
    )(ac, as_, cs)
    return out.reshape(B, n2 * n1, W)


def _two_stage_dft_ok(T):
    n = math.isqrt(T)
    return n * n == T and n % BF16_SUBLANES == 0


def fourier_branch(f, cs_chan, tm, B, T, scale):
    W = f.shape[-1]
    if _two_stage_dft_ok(T):
        n = math.isqrt(T)
        zc, zs = chan_dft(f, cs_chan, tm, F32)
        ac, as_ = dft_stage1(zc.reshape(B, T, W), zs.reshape(B, T, W), n, n)
        return dft_stage2(ac, as_, scale)
    zc, zs = chan_dft(f, cs_chan, tm, BF)
    return time_dft_dense(zc.reshape(B, T, W), zs.reshape(B, T, W), scale)


def _merge_kernel(b0_ref, b1_ref, b2_ref, w_ref, g0_ref, g1_ref, g2_ref, o_ref):
    acc = g0_ref[...] * _dot(b0_ref[...], w_ref[0])
    acc = acc + g1_ref[...] * _dot(b1_ref[...], w_ref[1])
    acc = acc + g2_ref[...] * _dot(b2_ref[...], w_ref[2])
    o_ref[...] = acc.astype(o_ref.dtype)


def gated_merge(branches, w_branch, w_idx, gates, tm):
    G, S, K = branches[0].shape
    tm = min(tm, 1024)
    tn = 256
    D = w_branch.shape[-1]
    nb = D // tn
    b_spec = pl.BlockSpec((None, tm, K), lambda g, m, n: (g, m, 0))

    def g_spec(j):
        return pl.BlockSpec((None, tm, tn), lambda g, m, n: (g, m, j * nb + n))

    return pl.pallas_call(
        _merge_kernel,
        out_shape=jax.ShapeDtypeStruct((G, S, D), BF),
        grid=(G, S // tm, nb),
        in_specs=[b_spec, b_spec, b_spec,
                  _w_spec(w_idx, (N_BRANCH, K, tn), lambda n: n),
                  g_spec(0), g_spec(1), g_spec(2)],
        out_specs=pl.BlockSpec((None, tm, tn), lambda g, m, n: (g, m, n)),
        compiler_params=_cparams("arbitrary", "arbitrary", "arbitrary"),
        name="gated_merge",
    )(*branches, w_branch, gates, gates, gates)


def _rope_tables(n_tokens):
    rows = n_tokens // GRID_W
    row = jnp.repeat(jnp.arange(rows, dtype=F32), GRID_W)
    col = jnp.tile(jnp.arange(GRID_W, dtype=F32), rows)
    axis_dim = HEAD_DIM // 2
    inv_freq = ROPE_THETA ** (-jnp.arange(0, axis_dim, 2, dtype=F32) / axis_dim)
    ar, ac = row[:, None] * inv_freq, col[:, None] * inv_freq
    cos = jnp.concatenate([jnp.cos(ar), jnp.cos(ar), jnp.cos(ac), jnp.cos(ac)], axis=-1)
    sin = jnp.concatenate([-jnp.sin(ar), jnp.sin(ar), -jnp.sin(ac), jnp.sin(ac)], axis=-1)
    return cos, sin


def kernel(x, c, ctx, c_ctx, w_ada, b_ada, g_norm, w_ff_in, w_ff_out, w_in, b_gate, q_gain, k_gain,
           conv_w, conv_b, lru_wa, lru_ba, lru_wx, lru_bx, lru_lam, w_branch, w_out, g_final):
    B, S, D = x.shape
    Tc = ctx.shape[1]
    depth = w_ada.shape[0]
    lru_w = conv_w.shape[-1]
    fnet_w = D // 2
    attn_w = (D // 256) * HEAD_DIM
    kv_w = attn_w // Q_GROUP
    v0 = attn_w + kv_w
    x0 = v0 + kv_w
    y0 = x0 + lru_w
    f0 = y0 + lru_w
    g0 = f0 + fnet_w

    lru_wa_b = lru_wa.astype(BF)
    lru_wx_b = lru_wx.astype(BF)

    n_rows = SUBLANES
    c_rows = jnp.concatenate([c, c_ctx[None, :], jnp.zeros((n_rows - B - 1, D), F32)], axis=0)
    mod = ada_mod(c_rows, w_ada, b_ada).reshape(depth, n_rows, N_MOD, 1, D)
    mod_lat = mod[:, :B]
    mod_ctx = mod[:, B:B + 1]

    cos_l, sin_l = _rope_tables(S)
    cos_c, sin_c = jnp.ones((B * Tc, HEAD_DIM), F32), jnp.zeros((B * Tc, HEAD_DIM), F32)
    q_fold = HEAD_DIM ** -0.5 * math.log2(math.e)
    gain_qk = jnp.concatenate([jnp.tile(q_gain * q_fold, (1, attn_w // HEAD_DIM)),
                               jnp.tile(k_gain, (1, kv_w // HEAD_DIM))], axis=1)
    gw = fnet_w // FNET_GROUPS
    cg, sg = _dft_tables(gw)
    cs_chan = jnp.concatenate([cg, sg], axis=1).astype(BF)

    tm_ffn = min(2048, S)
    tm_l = min(1024, S)
    tm_c = B * Tc
    xl = x
    xc = ctx.reshape(1, B * Tc, D)
    mods = {"lat": mod_lat, "ctx": mod_ctx}

    wts = {}
    for l in range(depth):
        wts["branch", l] = (w_branch, (l,))
        wts["out", l] = (w_out, (l,))
        for i in range(2):
            wts["ff_in", l, i] = (w_ff_in, (l, i))
            wts["ff_out", l, i] = (w_ff_out, (l, i))
    w_branch_rows = w_branch.reshape(depth, N_BRANCH * w_branch.shape[2], D)

    def ffn(xs, which, l, i, k_mod):
        lat = which == "lat"
        mod_l = mods[which][l]
        h = norm_mod(xs, g_norm[l, 2 * i], mod_l, k_mod, k_mod + 1)
        act = swiglu_in(h, *wts["ff_in", l, i], tm_ffn if lat else tm_c)
        return resid_out(act, *wts["ff_out", l, i], xs, mod_l, k_mod + 2, 0.5, tm_l if lat else tm_c)

    for l in range(depth):
        last = l == depth - 1
        xl = ffn(xl, "lat", l, 0, 0)
        xc = ffn(xc, "ctx", l, 0, 0)

        hl = norm_mod(xl, g_norm[l, 1], mod_lat[l], 3, 4)
        hc = norm_mod(xc, g_norm[l, 1], mod_ctx[l], 3, 4)
        gain = gain_qk[l:l + 1]
        bias_g = b_gate[l].reshape(1, N_BRANCH * D)

        qk_c = proj_qk(hc, w_in, l, attn_w + kv_w, gain, cos_c, sin_c, B * Tc, tm_c).reshape(B, Tc, -1)
        v_c = proj(hc, w_in, l, v0, kv_w, BF, tm_c).reshape(B, Tc, kv_w)
        xs_c = proj(hc, w_in, l, x0, lru_w, F32, tm_c).reshape(B, Tc, lru_w)
        if last:
            gy_c = jnp.zeros((B, Tc, lru_w), F32)
        else:
            gy_c = proj(hc, w_in, l, y0, lru_w, F32, tm_c, act="gelu").reshape(B, Tc, lru_w)
        lru_args = (conv_w, conv_b, lru_wa_b, lru_ba, lru_wx_b, lru_bx, lru_lam, l)
        lru_c, h_fin = lru_mix(xs_c, gy_c, jnp.zeros((B, 2, lru_w), F32), *lru_args)

        qk_l = proj_qk(hl, w_in, l, attn_w + kv_w, gain, cos_l, sin_l, S, tm_l)
        v_l, xs_l, gy_l, f_l, gate_l = proj_segments(
            hl, w_in, l, v0, ((kv_w, BF, "none"), (lru_w, F32, "none"), (lru_w, F32, "gelu"), (fnet_w, BF, "none"),
                              (N_BRANCH * D, F32, "sigmoid_bias")), bias_g, tm_l)

        cast_keys = [("branch", l), ("out", l), ("ff_in", l, 1), ("ff_out", l, 1)]
        if not last:
            cast_keys += [("ff_in", l + 1, 0), ("ff_out", l + 1, 0)]
        cast = [(w_branch_rows, (l,)) if key[0] == "branch" else wts[key] for key in cast_keys]
        attn_l, copies = attention(qk_l, v_l, qk_c, v_c, attn_w, True, cast=cast)
        for key, w_bf in zip(cast_keys, copies):
            if key[0] == "branch":
                w_bf = w_bf.reshape(w_branch.shape[1:])
            wts[key] = (w_bf, ())

        lru_l, _ = lru_mix(xs_l, gy_l, h_fin, *lru_args)
        four_l = fourier_branch(f_l, cs_chan, tm_l, B, S, 1.0 / math.sqrt(S * gw))
        merged = gated_merge((attn_l, lru_l, four_l), *wts["branch", l], gate_l, tm_l)
        xl = resid_out(merged, *wts["out", l], xl, mod_lat[l], 5, 1.0, tm_l)
        xl = ffn(xl, "lat", l, 1, 6)

        if not last:
            f_c = proj(hc, w_in, l, f0, fnet_w, BF, tm_c)
            gate_c = proj(hc, w_in, l, g0, N_BRANCH * D, F32, tm_c, act="sigmoid_bias", bias=bias_g)
            attn_c, _ = attention(qk_c, v_c, qk_c, v_c, attn_w, False)
            four_c = fourier_branch(f_c, cs_chan, tm_c, B, Tc, 1.0 / math.sqrt(Tc * gw))
            merged_c = gated_merge((attn_c.reshape(1, B * Tc, attn_w), lru_c.reshape(1, B * Tc, lru_w),
                                    four_c.reshape(1, B * Tc, fnet_w)), *wts["branch", l], gate_c, tm_c)
            xc = resid_out(merged_c, *wts["out", l], xc, mod_ctx[l], 5, 1.0, tm_c)
            xc = ffn(xc, "ctx", l, 1, 6)

    return final_norm(xl, g_final)
```

```python
import functools
import math

import jax
import jax.numpy as jnp
from jax import lax
from jax.experimental import pallas as pl
from jax.experimental.pallas import tpu as pltpu

F32 = jnp.float32
BF = jnp.bfloat16

VMEM_LIMIT_BYTES = 56 * 1024 * 1024
ATTENTION_VMEM_LIMIT_BYTES = 60 * 1024 * 1024
LANES = 128
SUBLANES = 8
BF16_SUBLANES = 16

HEAD_DIM = 128
Q_GROUP = 4
GRID_W = 64
CONV_W = 4
CONV_LEFT = 2
LRU_C = 8.0
LRU_BLOCK = 128
FNET_GROUPS = 4
N_BRANCH = 3
N_MOD = 9
ROPE_THETA = 10000.0
EPS = 1e-6
CONV_PAD = 8


def _cparams(*sem):
    return pltpu.CompilerParams(dimension_semantics=sem, vmem_limit_bytes=VMEM_LIMIT_BYTES)


def _silu(a):
    return a * jax.nn.sigmoid(a)


MATMUL_VMEM_BUDGET = 55 * 1024 * 1024


def _plan_tiles(tm_max, K, w_dtype, w_tiles, io_bytes_per_elem, tmp_bytes_per_elem, extra_bytes_per_row=0,
                tn_options=(512, 256)):
    w_bytes = jnp.dtype(w_dtype).itemsize
    w_copy_bytes = 0 if w_dtype == BF else 2
    tm = tm_max
    while tm >= 256:
        for tn in tn_options:
            need = (2 * tm * K * 2 + w_tiles * K * tn * (2 * w_bytes + w_copy_bytes)
                    + tm * tn * (2 * io_bytes_per_elem + tmp_bytes_per_elem) + tm * extra_bytes_per_row)
            if need <= MATMUL_VMEM_BUDGET:
                return tm, tn
        tm //= 2
    raise ValueError("no matmul tiling fits VMEM")


def _lhs_spec(tm, K):
    return pl.BlockSpec((None, tm, K), lambda g, m, n: (g, m, 0))


def _dot(a, w):
    return jnp.dot(a, w.astype(BF), preferred_element_type=F32)


EPILOGUE_ROWS = 256


def _row_chunks(tm):
    rows = min(EPILOGUE_ROWS, tm)
    return [slice(c * rows, (c + 1) * rows) for c in range(tm // rows)]


def _ada_kernel(c_ref, w_ref, b_ref, o_ref):
    s = _silu(c_ref[...]).astype(BF)
    o_ref[...] = _dot(s, w_ref[...]) + b_ref[...]


def ada_mod(c_rows, w_ada, b_ada, tn=1024):
    depth, d, n = w_ada.shape
    rows = c_rows.shape[0]
    return pl.pallas_call(
        _ada_kernel,
        out_shape=jax.ShapeDtypeStruct((depth, rows, n), F32),
        grid=(depth, n // tn),
        in_specs=[pl.BlockSpec((rows, d), lambda l, j: (0, 0)),
                  pl.BlockSpec((None, d, tn), lambda l, j: (l, 0, j)),
                  pl.BlockSpec((None, 1, tn), lambda l, j: (l, 0, j))],
        out_specs=pl.BlockSpec((None, rows, tn), lambda l, j: (l, 0, j)),
        compiler_params=_cparams("arbitrary", "arbitrary"),
        name="ada_mod",
    )(c_rows, w_ada, b_ada.reshape(depth, 1, n))


def _norm_mod_kernel(x_ref, g_ref, sh_ref, sc_ref, o_ref):
    x = x_ref[...]
    ms = jnp.mean(x * x, axis=-1, keepdims=True)
    y = x_ref[...] * lax.rsqrt(ms + EPS) * g_ref[...]
    o_ref[...] = (y * (1.0 + sc_ref[...]) + sh_ref[...]).astype(o_ref.dtype)


def norm_mod(x, g, mod, k_shift, k_scale, ts=512):
    G, S, D = x.shape
    ts = min(ts, S)
    return pl.pallas_call(
        _norm_mod_kernel,
        out_shape=jax.ShapeDtypeStruct((G, S, D), BF),
        grid=(G, S // ts),
        in_specs=[pl.BlockSpec((None, ts, D), lambda g_, i: (g_, i, 0)),
                  pl.BlockSpec((1, D), lambda g_, i: (0, 0)),
                  pl.BlockSpec((None, None, 1, D), lambda g_, i: (g_, k_shift, 0, 0)),
                  pl.BlockSpec((None, None, 1, D), lambda g_, i: (g_, k_scale, 0, 0))],
        out_specs=pl.BlockSpec((None, ts, D), lambda g_, i: (g_, i, 0)),
        compiler_params=_cparams("arbitrary", "arbitrary"),
        name="norm_mod",
    )(x, g.reshape(1, D), mod, mod)


def _final_norm_kernel(x_ref, g_ref, o_ref):
    x = x_ref[...]
    ms = jnp.mean(x * x, axis=-1, keepdims=True)
    o_ref[...] = x * lax.rsqrt(ms + EPS) * g_ref[...]


def final_norm(x, g, ts=512):
    G, S, D = x.shape
    ts = min(ts, S)
    return pl.pallas_call(
        _final_norm_kernel,
        out_shape=jax.ShapeDtypeStruct((G, S, D), F32),
        grid=(G, S // ts),
        in_specs=[pl.BlockSpec((None, ts, D), lambda g_, i: (g_, i, 0)),
                  pl.BlockSpec((1, D), lambda g_, i: (0, 0))],
        out_specs=pl.BlockSpec((None, ts, D), lambda g_, i: (g_, i, 0)),
        compiler_params=_cparams("arbitrary", "arbitrary"),
        name="final_norm",
    )(x, g.reshape(1, D))


def _swiglu_kernel(h_ref, wa_ref, wb_ref, o_ref):
    wa = wa_ref[...].astype(BF)
    wb = wb_ref[...].astype(BF)
    for r in _row_chunks(h_ref.shape[0]):
        h = h_ref[r, :]
        a = _dot(h, wa)
        b = _dot(h, wb)
        o_ref[r, :] = (_silu(a) * b).astype(o_ref.dtype)


def _w_spec(w_idx, block, col_block):
    lead = tuple(w_idx)
    return pl.BlockSpec((None,) * len(lead) + block,
                        lambda g, m, n: lead + (0,) * (len(block) - 1) + (col_block(n),))


def swiglu_in(h, w, w_idx, tm):
    G, S, K = h.shape
    F = w.shape[-1] // 2
    tm, tn = _plan_tiles(tm, K, w.dtype, 2, 2, 0, tn_options=(256,))
    nb = F // tn
    return pl.pallas_call(
        _swiglu_kernel,
        out_shape=jax.ShapeDtypeStruct((G, S, F), BF),
        grid=(G, S // tm, nb),
        in_specs=[_lhs_spec(tm, K),
                  _w_spec(w_idx, (K, tn), lambda n: n),
                  _w_spec(w_idx, (K, tn), lambda n: n + nb)],
        out_specs=pl.BlockSpec((None, tm, tn), lambda g, m, n: (g, m, n)),
        compiler_params=_cparams("arbitrary", "arbitrary", "arbitrary"),
        name="swiglu_in",
    )(h, w, w)


def _resid_kernel(coef, a_ref, w_ref, x_ref, g_ref, o_ref):
    y = _dot(a_ref[...], w_ref[...])
    o_ref[...] = x_ref[...] + (coef * g_ref[...]) * y


def resid_out(a, w, w_idx, x, mod, k_gate, coef, tm):
    G, S, K = a.shape
    D = x.shape[-1]
    tm, tn = _plan_tiles(tm, K, w.dtype, 1, 8, 16)
    return pl.pallas_call(
        functools.partial(_resid_kernel, coef),
        out_shape=jax.ShapeDtypeStruct((G, S, D), F32),
        grid=(G, S // tm, D // tn),
        in_specs=[_lhs_spec(tm, K),
                  _w_spec(w_idx, (K, tn), lambda n: n),
                  pl.BlockSpec((None, tm, tn), lambda g, m, n: (g, m, n)),
                  pl.BlockSpec((None, None, 1, tn), lambda g, m, n: (g, k_gate, 0, n))],
        out_specs=pl.BlockSpec((None, tm, tn), lambda g, m, n: (g, m, n)),
        compiler_params=_cparams("arbitrary", "arbitrary", "arbitrary"),
        name="resid_out",
    )(a, w, x, mod)


def _proj_kernel(act, h_ref, w_ref, *rest):
    o_ref = rest[-1]
    w = w_ref[...].astype(BF)
    for r in _row_chunks(h_ref.shape[0]):
        acc = _dot(h_ref[r, :], w)
        if act == "gelu":
            acc = jax.nn.gelu(acc)
        elif act == "sigmoid_bias":
            acc = jax.nn.sigmoid(acc + rest[0][...])
        o_ref[r, :] = acc.astype(o_ref.dtype)


def proj(h, w_in, l, col0, ncols, out_dtype, tm, act="none", bias=None):
    G, S, K = h.shape
    tm, tn = _plan_tiles(tm, K, w_in.dtype, 1, jnp.dtype(out_dtype).itemsize, 8)
    cb = col0 // tn
    in_specs = [_lhs_spec(tm, K),
                pl.BlockSpec((None, K, tn), lambda g, m, n: (l, 0, cb + n))]
    args = [h, w_in]
    if bias is not None:
        in_specs.append(pl.BlockSpec((1, tn), lambda g, m, n: (0, n)))
        args.append(bias)
    return pl.pallas_call(
        functools.partial(_proj_kernel, act),
        out_shape=jax.ShapeDtypeStruct((G, S, ncols), out_dtype),
        grid=(G, S // tm, ncols // tn),
        in_specs=in_specs,
        out_specs=pl.BlockSpec((None, tm, tn), lambda g, m, n: (g, m, n)),
        compiler_params=_cparams("arbitrary", "arbitrary", "arbitrary"),
        name="proj_" + act,
    )(*args)


def _proj_segments_kernel(segments, h_ref, w_ref, bias_ref, *o_refs):
    n = pl.program_id(2)
    for (lo, hi, act), o_ref in zip(segments, o_refs):
        @pl.when((n >= lo) & (n < hi))
        def _(act=act, o_ref=o_ref):
            w = w_ref[...].astype(BF)
            for r in _row_chunks(h_ref.shape[0]):
                acc = _dot(h_ref[r, :], w)
                if act == "gelu":
                    acc = jax.nn.gelu(acc)
                elif act == "sigmoid_bias":
                    acc = jax.nn.sigmoid(acc + bias_ref[...])
                o_ref[r, :] = acc.astype(o_ref.dtype)


def proj_segments(h, w_in, l, col0, segments, bias, tm, tn=512):
    G, S, K = h.shape
    bias_lo, bias_tiles = 0, 1
    if bias is None:
        bias = jnp.zeros((1, tn), F32)
    bounds, lo = [], 0
    for ncols, _, act in segments:
        bounds.append((lo, lo + ncols // tn, act))
        if act == "sigmoid_bias":
            bias_lo, bias_tiles = lo, ncols // tn
        lo += ncols // tn
    cb = col0 // tn

    def out_spec(lo, hi):
        return pl.BlockSpec((None, tm, tn), lambda g, m, n: (g, m, jnp.clip(n - lo, 0, hi - lo - 1)))

    return pl.pallas_call(
        functools.partial(_proj_segments_kernel, tuple(bounds)),
        out_shape=tuple(jax.ShapeDtypeStruct((G, S, ncols), dt) for ncols, dt, _ in segments),
        grid=(G, S // tm, lo),
        in_specs=[_lhs_spec(tm, K),
                  pl.BlockSpec((None, K, tn), lambda g, m, n: (l, 0, cb + n)),
                  pl.BlockSpec((1, tn), lambda g, m, n: (0, jnp.clip(n - bias_lo, 0, bias_tiles - 1)))],
        out_specs=tuple(out_spec(a, b) for a, b, _ in bounds),
        compiler_params=_cparams("arbitrary", "arbitrary", "arbitrary"),
        name="proj_segments",
    )(h, w_in, bias)


def _qk_kernel(h_ref, w_ref, gain_ref, cos_ref, sin_ref, swap_ref, o_ref):
    acc = _dot(h_ref[...], w_ref[...])
    cos = cos_ref[...]
    sin = sin_ref[...]
    swap = swap_ref[...]
    for hd in range(acc.shape[1] // HEAD_DIM):
        cols = slice(hd * HEAD_DIM, (hd + 1) * HEAD_DIM)
        v = acc[:, cols]
        ms = jnp.mean(v * v, axis=-1, keepdims=True)
        y = v * lax.rsqrt(ms + EPS) * gain_ref[:, cols]
        hi = y.astype(BF)
        lo = (y - hi.astype(F32)).astype(BF)
        partner = (jnp.dot(hi, swap, preferred_element_type=F32)
                   + jnp.dot(lo, swap, preferred_element_type=F32))
        o_ref[:, cols] = (y * cos + partner * sin).astype(o_ref.dtype)


def proj_qk(h, w_in, l, ncols, gain, cos, sin, rows_per_seq, tm):
    G, S, K = h.shape
    tm, tn = _plan_tiles(tm, K, w_in.dtype, 1, 2, 8, extra_bytes_per_row=2 * 2 * HEAD_DIM * 4)
    seq_blocks = rows_per_seq // tm
    lane = jnp.arange(HEAD_DIM)
    quarter = HEAD_DIM // 4
    partner_lane = jnp.where((lane % (2 * quarter)) < quarter, lane + quarter, lane - quarter)
    swap = (lane[:, None] == partner_lane[None, :]).astype(BF)
    return pl.pallas_call(
        _qk_kernel,
        out_shape=jax.ShapeDtypeStruct((G, S, ncols), BF),
        grid=(G, S // tm, ncols // tn),
        in_specs=[_lhs_spec(tm, K),
                  pl.BlockSpec((None, K, tn), lambda g, m, n: (l, 0, n)),
                  pl.BlockSpec((1, tn), lambda g, m, n: (0, n)),
                  pl.BlockSpec((tm, HEAD_DIM), lambda g, m, n: (m % seq_blocks, 0)),
                  pl.BlockSpec((tm, HEAD_DIM), lambda g, m, n: (m % seq_blocks, 0)),
                  pl.BlockSpec((HEAD_DIM, HEAD_DIM), lambda g, m, n: (0, 0))],
        out_specs=pl.BlockSpec((None, tm, tn), lambda g, m, n: (g, m, n)),
        compiler_params=_cparams("arbitrary", "arbitrary", "arbitrary"),
        name="proj_qk",
    )(h, w_in, gain, cos, sin, swap)


def _key_split(n_keys):
    if n_keys < 3 * LANES:
        return 0
    blocks = (n_keys // 2) // LANES
    if blocks % 2 == 0:
        blocks -= 1
    return blocks * LANES


def _attn_kernel(with_latent_keys, n_cast, q_ref, kc_ref, vc_ref, *rest):
    n_in = 2 if with_latent_keys else 0
    cast_in = rest[n_in:n_in + n_cast]
    o_ref = rest[n_in + n_cast]
    cast_out = rest[n_in + n_cast + 1:]
    for w_src, w_dst in zip(cast_in, cast_out):
        w_dst[...] = w_src[...].astype(w_dst.dtype)
    if with_latent_keys:
        k = jnp.concatenate([kc_ref[...], rest[0][...]], axis=0)
        v = jnp.concatenate([vc_ref[...], rest[1][...]], axis=0)
    else:
        k, v = kc_ref[...], vc_ref[...]
    v_ones = jnp.concatenate([v, jnp.ones_like(v)], axis=1)
    split = _key_split(k.shape[0])
    for g in range(Q_GROUP):
        cols = slice(g * HEAD_DIM, (g + 1) * HEAD_DIM)
        s = lax.dot_general(q_ref[:, cols], k, (((1,), (1,)), ((), ())), preferred_element_type=F32)
        m = jnp.max(s, axis=-1, keepdims=True)
        if split:
            p_lo = jnp.exp2(s[:, :split] - m).astype(BF)
            p_hi = jnp.exp2(s[:, split:] - m).astype(BF)
            acc = (jnp.dot(p_lo, v_ones[:split], preferred_element_type=F32)
                   + jnp.dot(p_hi, v_ones[split:], preferred_element_type=F32))
        else:
            acc = jnp.dot(jnp.exp2(s - m).astype(BF), v_ones, preferred_element_type=F32)
        o_ref[:, cols] = (acc[:, :HEAD_DIM] / acc[:, HEAD_DIM:]).astype(o_ref.dtype)


def attention(qk, v, qk_ctx, v_ctx, attn_width, with_latent_keys, cast=(), tq=512):
    B, S, _ = qk.shape
    Tc = qk_ctx.shape[1]
    n_kv = v.shape[-1] // HEAD_DIM
    gw = Q_GROUP * HEAD_DIM
    kcol = attn_width // HEAD_DIM
    tq = min(tq, S)
    n_q = S // tq
    n_steps = B * n_kv * n_q
    in_specs = [pl.BlockSpec((None, tq, gw), lambda b, h, i: (b, i, h)),
                pl.BlockSpec((None, Tc, HEAD_DIM), lambda b, h, i: (b, 0, kcol + h)),
                pl.BlockSpec((None, Tc, HEAD_DIM), lambda b, h, i: (b, 0, h))]
    args = [qk, qk_ctx, v_ctx]
    if with_latent_keys:
        in_specs += [pl.BlockSpec((None, S, HEAD_DIM), lambda b, h, i: (b, 0, kcol + h)),
                     pl.BlockSpec((None, S, HEAD_DIM), lambda b, h, i: (b, 0, h))]
        args += [qk, v]
    out_shape = [jax.ShapeDtypeStruct((B, S, attn_width), BF)]
    out_specs = [pl.BlockSpec((None, tq, gw), lambda b, h, i: (b, i, h))]

    def step(b, h, i):
        return (b * n_kv + h) * n_q + i

    for w, w_idx in cast:
        rows, cols = w.shape[-2:]
        slab = rows // n_steps
        assert slab * n_steps == rows and slab % BF16_SUBLANES == 0, (rows, n_steps)
        lead = tuple(w_idx)
        in_specs.append(pl.BlockSpec((None,) * len(lead) + (slab, cols),
                                     lambda b, h, i, lead=lead: lead + (step(b, h, i), 0)))
        args.append(w)
        out_shape.append(jax.ShapeDtypeStruct((rows, cols), BF))
        out_specs.append(pl.BlockSpec((slab, cols), lambda b, h, i: (step(b, h, i), 0)))
    outs = pl.pallas_call(
        functools.partial(_attn_kernel, with_latent_keys, len(cast)),
        out_shape=tuple(out_shape),
        grid=(B, n_kv, n_q),
        in_specs=in_specs,
        out_specs=tuple(out_specs),
        compiler_params=pltpu.CompilerParams(dimension_semantics=("arbitrary",) * 3,
                                             vmem_limit_bytes=ATTENTION_VMEM_LIMIT_BYTES),
        name="attention",
    )(*args)
    return outs[0], list(outs[1:])


def _one_minus_sq(a, y):
    poly = y * (1.0 / 120.0) + (1.0 / 24.0)
    for coef in (1.0 / 6.0, 0.5, 1.0):
        poly = poly * y + coef
    return jnp.where(y > -0.0625, -(poly * y), 1.0 - a * a)


def _tile_scan(a, b, reverse):
    row = lax.broadcasted_iota(jnp.int32, a.shape, 0)
    for d in (1, 2, 4):
        shift = SUBLANES - d if reverse else d
        valid = (row < SUBLANES - d) if reverse else (row >= d)
        a_prev = jnp.where(valid, pltpu.roll(a, shift, 0), 1.0)
        b_prev = jnp.where(valid, pltpu.roll(b, shift, 0), 0.0)
        b = a * b_prev + b
        a = a * a_prev
    return a, b


def _lru_kernel(S, tchunk, unroll, x_ref, gy_ref, h0_ref, cw_ref, cb_ref, wa_ref, ba_ref, wx_ref, bx_ref,
                lam_ref, o_ref, hfin_ref, xs_ref, a_ref, b_ref, h_ref):
    zeros = jnp.zeros((CONV_PAD, LRU_BLOCK), F32)
    xs_ref[0:CONV_PAD, :] = zeros
    xs_ref[CONV_PAD + S:2 * CONV_PAD + S, :] = zeros
    xs_ref[CONV_PAD:CONV_PAD + S, :] = x_ref[...]

    neg_lam = -lam_ref[...]
    softplus = jnp.maximum(neg_lam, 0.0) + jnp.log1p(jnp.exp(-jnp.abs(neg_lam)))
    log_a_per_r = -LRU_C * softplus
    log2_a_per_r = log_a_per_r * math.log2(math.e)

    for c in range(S // tchunk):
        t0 = c * tchunk
        xc = cb_ref[...]
        for j in range(CONV_W):
            xc = xc + xs_ref[pl.ds(CONV_PAD + t0 + j - CONV_LEFT, tchunk), :] * cw_ref[j:j + 1, :]
        xb = xc.astype(BF)
        for d in range(2):
            r = jax.nn.sigmoid(jnp.dot(xb, wa_ref[d], preferred_element_type=F32) + ba_ref[d:d + 1, :])
            i = jax.nn.sigmoid(jnp.dot(xb, wx_ref[d], preferred_element_type=F32) + bx_ref[d:d + 1, :])
            a = jnp.exp2(r * log2_a_per_r[d:d + 1, :])
            a_ref[d, pl.ds(t0, tchunk), :] = a
            one_minus_a2 = _one_minus_sq(a, r * (2.0 * log_a_per_r[d:d + 1, :]))
            b_ref[d, pl.ds(t0, tchunk), :] = jnp.sqrt(one_minus_a2) * (i * xc)

    n_tiles = S // SUBLANES

    def body(i, carry):
        hf, hr = carry
        tf = pl.multiple_of(i * SUBLANES, SUBLANES)
        af, bf = _tile_scan(a_ref[0, pl.ds(tf, SUBLANES), :], b_ref[0, pl.ds(tf, SUBLANES), :], False)
        h = bf + af * hf
        h_ref[0, pl.ds(tf, SUBLANES), :] = h
        hf = h[SUBLANES - 1:SUBLANES, :]
        tr = pl.multiple_of((n_tiles - 1 - i) * SUBLANES, SUBLANES)
        ar, br = _tile_scan(a_ref[1, pl.ds(tr, SUBLANES), :], b_ref[1, pl.ds(tr, SUBLANES), :], True)
        h = br + ar * hr
        h_ref[1, pl.ds(tr, SUBLANES), :] = h
        hr = h[0:1, :]
        return hf, hr

    hf, hr = lax.fori_loop(0, n_tiles, body, (h0_ref[0:1, :], h0_ref[1:2, :]), unroll=unroll)
    hfin_ref[0:1, :] = hf
    hfin_ref[1:2, :] = hr
    o_ref[...] = ((h_ref[0] + h_ref[1]) * gy_ref[...]).astype(o_ref.dtype)


def lru_mix(xseg, gy, h0, conv_w, conv_b, wa, ba, wx, bx, lam, l, tchunk=512, unroll=4):
    B, S, W = xseg.shape
    tchunk = min(tchunk, S)
    nblk = W // LRU_BLOCK
    seq = pl.BlockSpec((None, S, LRU_BLOCK), lambda b, n: (b, 0, n))
    st = pl.BlockSpec((None, 2, LRU_BLOCK), lambda b, n: (b, 0, n))
    vec2 = pl.BlockSpec((None, 2, LRU_BLOCK), lambda b, n: (l, 0, n))
    wblk = pl.BlockSpec((None, 2, None, LRU_BLOCK, LRU_BLOCK), lambda b, n: (l, 0, n, 0, 0))
    return pl.pallas_call(
        functools.partial(_lru_kernel, S, tchunk, unroll),
        out_shape=(jax.ShapeDtypeStruct((B, S, W), BF), jax.ShapeDtypeStruct((B, 2, W), F32)),
        grid=(B, nblk),
        in_specs=[seq, seq, st,
                  pl.BlockSpec((None, CONV_W, LRU_BLOCK), lambda b, n: (l, 0, n)),
                  pl.BlockSpec((None, 1, LRU_BLOCK), lambda b, n: (l, 0, n)),
                  wblk, vec2, wblk, vec2, vec2],
        out_specs=(seq, st),
        scratch_shapes=[pltpu.VMEM((S + 2 * CONV_PAD, LRU_BLOCK), F32),
                        pltpu.VMEM((2, S, LRU_BLOCK), F32),
                        pltpu.VMEM((2, S, LRU_BLOCK), F32),
                        pltpu.VMEM((2, S, LRU_BLOCK), F32)],
        compiler_params=_cparams("arbitrary", "arbitrary"),
        name="lru_mix",
    )(xseg, gy, h0, conv_w, conv_b.reshape(conv_b.shape[0], 1, W), wa, ba, wx, bx, lam)


def _dft_tables(n):
    k = jnp.arange(n, dtype=jnp.int32)
    ang = ((k[:, None] * k[None, :]) % n).astype(F32) * (2.0 * math.pi / n)
    return jnp.cos(ang), jnp.sin(ang)


def _chan_dft_kernel(f_ref, cs_ref, zc_ref, zs_ref):
    w = zc_ref.shape[-1]
    z = jnp.dot(f_ref[...], cs_ref[...], preferred_element_type=F32)
    zc_ref[...] = z[:, :w].astype(zc_ref.dtype)
    zs_ref[...] = z[:, w:].astype(zs_ref.dtype)


def chan_dft(f, cs, tm, out_dtype):
    G, S, W = f.shape
    gw = W // FNET_GROUPS
    blk = pl.BlockSpec((None, tm, gw), lambda g, m, n: (g, m, n))
    return pl.pallas_call(
        _chan_dft_kernel,
        out_shape=(jax.ShapeDtypeStruct((G, S, W), out_dtype),) * 2,
        grid=(G, S // tm, FNET_GROUPS),
        in_specs=[blk, pl.BlockSpec((gw, 2 * gw), lambda g, m, n: (0, 0))],
        out_specs=(blk, blk),
        compiler_params=_cparams("arbitrary", "arbitrary", "arbitrary"),
        name="chan_dft",
    )(f, cs)


def _time_dft_kernel(scale, c_ref, s_ref, zc_ref, zs_ref, o_ref):
    re = jnp.dot(c_ref[...], zc_ref[...], preferred_element_type=F32)
    re = re - jnp.dot(s_ref[...], zs_ref[...], preferred_element_type=F32)
    o_ref[...] = (re * scale).astype(o_ref.dtype)


def time_dft_dense(zc, zs, scale, tm=512, tn=512):
    B, T, W = zc.shape
    tm = min(tm, T)
    ct, st = (t.astype(BF) for t in _dft_tables(T))
    a_spec = pl.BlockSpec((tm, T), lambda b, n, m: (m, 0))
    z_spec = pl.BlockSpec((None, T, tn), lambda b, n, m: (b, 0, n))
    return pl.pallas_call(
        functools.partial(_time_dft_kernel, scale),
        out_shape=jax.ShapeDtypeStruct((B, T, W), BF),
        grid=(B, W // tn, T // tm),
        in_specs=[a_spec, a_spec, z_spec, z_spec],
        out_specs=pl.BlockSpec((None, tm, tn), lambda b, n, m: (b, m, n)),
        compiler_params=_cparams("arbitrary", "arbitrary", "arbitrary"),
        name="time_dft",
    )(ct, st, zc, zs)


def _dft_stage1_kernel(zc_ref, zs_ref, rot_ref, twc_ref, tws_ref, oc_ref, os_ref):
    n1 = zc_ref.shape[0]
    tn = zc_ref.shape[-1]
    rot = rot_ref[...]
    for j in range(zc_ref.shape[1]):
        z = jnp.concatenate([zc_ref[:, j, :], zs_ref[:, j, :]], axis=0).astype(BF)
        res = jnp.dot(rot, z, preferred_element_type=F32)
        ac, as_ = res[:n1], res[n1:]
        twc = jnp.tile(twc_ref[j], (1, tn // LANES))
        tws = jnp.tile(tws_ref[j], (1, tn // LANES))
        oc_ref[j] = (ac * twc - as_ * tws).astype(oc_ref.dtype)
        os_ref[j] = (as_ * twc + ac * tws).astype(os_ref.dtype)


def dft_stage1(zc, zs, n1, n2, tn=512):
    B, T, W = zc.shape
    c, s = _dft_tables(n1)
    rot = jnp.concatenate([jnp.concatenate([c, -s], axis=1), jnp.concatenate([s, c], axis=1)], axis=0).astype(BF)
    cc = jnp.arange(n1, dtype=jnp.int32)
    bb = jnp.arange(n2, dtype=jnp.int32)
    ang = (bb[:, None] * cc[None, :]).astype(F32) * (2.0 * math.pi / T)
    twc = jnp.broadcast_to(jnp.cos(ang)[:, :, None], (n2, n1, LANES))
    tws = jnp.broadcast_to(jnp.sin(ang)[:, :, None], (n2, n1, LANES))
    z_spec = pl.BlockSpec((None, n1, SUBLANES, tn), lambda b, i, n: (b, 0, i, n))
    tw_spec = pl.BlockSpec((SUBLANES, n1, LANES), lambda b, i, n: (i, 0, 0))
    o_spec = pl.BlockSpec((None, SUBLANES, n1, tn), lambda b, i, n: (b, i, 0, n))
    return pl.pallas_call(
        _dft_stage1_kernel,
        out_shape=(jax.ShapeDtypeStruct((B, n2, n1, W), BF),) * 2,
        grid=(B, n2 // SUBLANES, W // tn),
        in_specs=[z_spec, z_spec, pl.BlockSpec((2 * n1, 2 * n1), lambda b, i, n: (0, 0)), tw_spec, tw_spec],
        out_specs=(o_spec, o_spec),
        compiler_params=_cparams("arbitrary", "arbitrary", "arbitrary"),
        name="dft_stage1",
    )(zc.reshape(B, n1, n2, W), zs.reshape(B, n1, n2, W), rot, twc, tws)


def _dft_stage2_kernel(scale, ac_ref, as_ref, cs_ref, o_ref, ac_f32, as_f32, o_f32):
    cs = cs_ref[...]
    ac_f32[...] = ac_ref[...].astype(F32)
    as_f32[...] = as_ref[...].astype(F32)
    for j in range(ac_ref.shape[1]):
        a = jnp.concatenate([ac_f32[:, j, :], as_f32[:, j, :]], axis=0).astype(BF)
        o_f32[:, j, :] = jnp.dot(cs, a, preferred_element_type=F32) * scale
    o_ref[...] = o_f32[...].astype(o_ref.dtype)


def dft_stage2(ac, as_, scale, tn=512):
    B, n2, n1, W = ac.shape
    c, s = _dft_tables(n2)
    cs = jnp.concatenate([c, -s], axis=1).astype(BF)
    cblk = BF16_SUBLANES
    a_spec = pl.BlockSpec((None, n2, cblk, tn), lambda b, i, n: (b, 0, i, n))
    out = pl.pallas_call(
        functools.partial(_dft_stage2_kernel, scale),
        out_shape=jax.ShapeDtypeStruct((B, n2, n1, W), BF),
        grid=(B, n1 // cblk, W // tn),
        in_specs=[a_spec, a_spec, pl.BlockSpec((n2, 2 * n2), lambda b, i, n: (0, 0))],
        out_specs=a_spec,
        scratch_shapes=[pltpu.VMEM((n2, cblk, tn), F32)] * 3,
        compiler_params=_cparams("arbitrary", "arbitrary", "arbitrary"),
        name="dft_stage2",
    )(ac, as_, cs)
    return out.reshape(B, n2 * n1, W)


def _two_stage_dft_ok(T):
    n = math.isqrt(T)
    return n * n == T and n % BF16_SUBLANES == 0


def fourier_branch(f, cs_chan, tm, B, T, scale):
    W = f.shape[-1]
    if _two_stage_dft_ok(T):
        n = math.isqrt(T)
        zc, zs = chan_dft(f, cs_chan, tm, F32)
        ac, as_ = dft_stage1(zc.reshape(B, T, W), zs.reshape(B, T, W), n, n)
        return dft_stage2(ac, as_, scale)
    zc, zs = chan_dft(f, cs_chan, tm, BF)
    return time_dft_dense(zc.reshape(B, T, W), zs.reshape(B, T, W), scale)


def _merge_kernel(b0_ref, b1_ref, b2_ref, w_ref, g0_ref, g1_ref, g2_ref, o_ref):
    acc = g0_ref[...] * _dot(b0_ref[...], w_ref[0])
    acc = acc + g1_ref[...] * _dot(b1_ref[...], w_ref[1])
    acc = acc + g2_ref[...] * _dot(b2_ref[...], w_ref[2])
    o_ref[...] = acc.astype(o_ref.dtype)


def gated_merge(branches, w_branch, w_idx, gates, tm):
    G, S, K = branches[0].shape
    tm = min(tm, 1024)
    tn = 256
    D = w_branch.shape[-1]
    nb = D // tn
    b_spec = pl.BlockSpec((None, tm, K), lambda g, m, n: (g, m, 0))

    def g_spec(j):
        return pl.BlockSpec((None, tm, tn), lambda g, m, n: (g, m, j * nb + n))

    return pl.pallas_call(
        _merge_kernel,
        out_shape=jax.ShapeDtypeStruct((G, S, D), BF),
        grid=(G, S // tm, nb),
        in_specs=[b_spec, b_spec, b_spec,
                  _w_spec(w_idx, (N_BRANCH, K, tn), lambda n: n),
                  g_spec(0), g_spec(1), g_spec(2)],
        out_specs=pl.BlockSpec((None, tm, tn), lambda g, m, n: (g, m, n)),
        compiler_params=_cparams("arbitrary", "arbitrary", "arbitrary"),
        name="gated_merge",
    )(*branches, w_branch, gates, gates, gates)


def _rope_tables(n_tokens):
    rows = n_tokens // GRID_W
    row = jnp.repeat(jnp.arange(rows, dtype=F32), GRID_W)
    col = jnp.tile(jnp.arange(GRID_W, dtype=F32), rows)
    axis_dim = HEAD_DIM // 2
    inv_freq = ROPE_THETA ** (-jnp.arange(0, axis_dim, 2, dtype=F32) / axis_dim)
    ar, ac = row[:, None] * inv_freq, col[:, None] * inv_freq
    cos = jnp.concatenate([jnp.cos(ar), jnp.cos(ar), jnp.cos(ac), jnp.cos(ac)], axis=-1)
    sin = jnp.concatenate([-jnp.sin(ar), jnp.sin(ar), -jnp.sin(ac), jnp.sin(ac)], axis=-1)
    return cos, sin


def kernel(x, c, ctx, c_ctx, w_ada, b_ada, g_norm, w_ff_in, w_ff_out, w_in, b_gate, q_gain, k_gain,
           conv_w, conv_b, lru_wa, lru_ba, lru_wx, lru_bx, lru_lam, w_branch, w_out, g_final):
    B, S, D = x.shape
    Tc = ctx.shape[1]
    depth = w_ada.shape[0]
    lru_w = conv_w.shape[-1]
    fnet_w = D // 2
    attn_w = (D // 256) * HEAD_DIM
    kv_w = attn_w // Q_GROUP
    v0 = attn_w + kv_w
    x0 = v0 + kv_w
    y0 = x0 + lru_w
    f0 = y0 + lru_w
    g0 = f0 + fnet_w

    lru_wa_b = lru_wa.astype(BF)
    lru_wx_b = lru_wx.astype(BF)

    n_rows = SUBLANES
    c_rows = jnp.concatenate([c, c_ctx[None, :], jnp.zeros((n_rows - B - 1, D), F32)], axis=0)
    mod = ada_mod(c_rows, w_ada, b_ada).reshape(depth, n_rows, N_MOD, 1, D)
    mod_lat = mod[:, :B]
    mod_ctx = mod[:, B:B + 1]

    cos_l, sin_l = _rope_tables(S)
    cos_c, sin_c = jnp.ones((B * Tc, HEAD_DIM), F32), jnp.zeros((B * Tc, HEAD_DIM), F32)
    q_fold = HEAD_DIM ** -0.5 * math.log2(math.e)
    gain_qk = jnp.concatenate([jnp.tile(q_gain * q_fold, (1, attn_w // HEAD_DIM)),
                               jnp.tile(k_gain, (1, kv_w // HEAD_DIM))], axis=1)
    gw = fnet_w // FNET_GROUPS
    cg, sg = _dft_tables(gw)
    cs_chan = jnp.concatenate([cg, sg], axis=1).astype(BF)

    tm_ffn = min(2048, S)
    tm_l = min(1024, S)
    tm_c = B * Tc
    xl = x
    xc = ctx.reshape(1, B * Tc, D)
    mods = {"lat": mod_lat, "ctx": mod_ctx}

    wts = {}
    for l in range(depth):
        wts["branch", l] = (w_branch, (l,))
        wts["out", l] = (w_out, (l,))
        for i in range(2):
            wts["ff_in", l, i] = (w_ff_in, (l, i))
            wts["ff_out", l, i] = (w_ff_out, (l, i))
    w_branch_rows = w_branch.reshape(depth, N_BRANCH * w_branch.shape[2], D)

    def ffn(xs, which, l, i, k_mod):
        lat = which == "lat"
        mod_l = mods[which][l]
        h = norm_mod(xs, g_norm[l, 2 * i], mod_l, k_mod, k_mod + 1)
        act = swiglu_in(h, *wts["ff_in", l, i], tm_ffn if lat else tm_c)
        return resid_out(act, *wts["ff_out", l, i], xs, mod_l, k_mod + 2, 0.5, tm_l if lat else tm_c)

    for l in range(depth):
        last = l == depth - 1
        xl = ffn(xl, "lat", l, 0, 0)
        xc = ffn(xc, "ctx", l, 0, 0)

        hl = norm_mod(xl, g_norm[l, 1], mod_lat[l], 3, 4)
        hc = norm_mod(xc, g_norm[l, 1], mod_ctx[l], 3, 4)
        gain = gain_qk[l:l + 1]
        bias_g = b_gate[l].reshape(1, N_BRANCH * D)

        qk_c = proj_qk(hc, w_in, l, attn_w + kv_w, gain, cos_c, sin_c, B * Tc, tm_c).reshape(B, Tc, -1)
        if last:
            v_c, xs_c = proj_segments(hc, w_in, l, v0, ((kv_w, BF, "none"), (lru_w, F32, "none")), None, tm_c)
            gy_c = jnp.zeros((B, Tc, lru_w), F32)
        else:
            v_c, xs_c, gy_c, f_c, gate_c = proj_segments(
                hc, w_in, l, v0, ((kv_w, BF, "none"), (lru_w, F32, "none"), (lru_w, F32, "gelu"),
                                  (fnet_w, BF, "none"), (N_BRANCH * D, F32, "sigmoid_bias")), bias_g, tm_c)
            gy_c = gy_c.reshape(B, Tc, lru_w)
        v_c = v_c.reshape(B, Tc, kv_w)
        xs_c = xs_c.reshape(B, Tc, lru_w)
        lru_args = (conv_w, conv_b, lru_wa_b, lru_ba, lru_wx_b, lru_bx, lru_lam, l)
        lru_c, h_fin = lru_mix(xs_c, gy_c, jnp.zeros((B, 2, lru_w), F32), *lru_args)

        qk_l = proj_qk(hl, w_in, l, attn_w + kv_w, gain, cos_l, sin_l, S, tm_l)
        v_l, xs_l, gy_l, f_l, gate_l = proj_segments(
            hl, w_in, l, v0, ((kv_w, BF, "none"), (lru_w, F32, "none"), (lru_w, F32, "gelu"), (fnet_w, BF, "none"),
                              (N_BRANCH * D, F32, "sigmoid_bias")), bias_g, tm_l)

        cast_keys = [("branch", l), ("out", l), ("ff_in", l, 1), ("ff_out", l, 1)]
        if not last:
            cast_keys += [("ff_in", l + 1, 0), ("ff_out", l + 1, 0)]
        cast = [(w_branch_rows, (l,)) if key[0] == "branch" else wts[key] for key in cast_keys]
        attn_l, copies = attention(qk_l, v_l, qk_c, v_c, attn_w, True, cast=cast)
        for key, w_bf in zip(cast_keys, copies):
            if key[0] == "branch":
                w_bf = w_bf.reshape(w_branch.shape[1:])
            wts[key] = (w_bf, ())

        lru_l, _ = lru_mix(xs_l, gy_l, h_fin, *lru_args)
        four_l = fourier_branch(f_l, cs_chan, tm_l, B, S, 1.0 / math.sqrt(S * gw))
        merged = gated_merge((attn_l, lru_l, four_l), *wts["branch", l], gate_l, tm_l)
        xl = resid_out(merged, *wts["out", l], xl, mod_lat[l], 5, 1.0, tm_l)
        xl = ffn(xl, "lat", l, 1, 6)

        if not last:
            attn_c, _ = attention(qk_c, v_c, qk_c, v_c, attn_w, False)
            four_c = fourier_branch(f_c, cs_chan, tm_c, B, Tc, 1.0 / math.sqrt(Tc * gw))
            merged_c = gated_merge((attn_c.reshape(1, B * Tc, attn_w), lru_c.reshape(1, B * Tc, lru_w),
                                    four_c.reshape(1, B * Tc, fnet_w)), *wts["branch", l], gate_c, tm_c)
            xc = resid_out(merged_c, *wts["out", l], xc, mod_ctx[l], 5, 1.0, tm_c)
            xc = ffn(xc, "ctx", l, 1, 6)

    return final_norm(xl, g_final)
```
